```python
import jax, jax.numpy as jnp
from jax import lax
import numpy as np

D_MODEL = 1024
BATCH = 8
SEQ = 4096
DEPTH = 4

CHUNK = 64
Q_BLOCK = 128
N_A_LAYERS = DEPTH // 2
MLA_HEADS = 8
QK_NOPE = 128
QK_ROPE = 64
V_HEAD = 128
Q_LORA = 384
KV_LORA = 256
ROPE_BASE = 10000.0
SB_HEADS = 8
SB_HEAD = 128
D_FF = 2816
N_EXPERTS = 8
TOP_K = 2
D_FF_EXPERT = 2816
EPS = 1e-6

kernel_name = "yoco_mla_stickbreaking_moe_trunk"


def rms_norm(x, g):
    xf = x.astype(jnp.float32)
    y = xf * lax.rsqrt(jnp.mean(xf * xf, axis=-1, keepdims=True) + EPS)
    return (y * g.astype(jnp.float32)).astype(x.dtype)


def rope_cos_sin(positions):
    inv_freq = 1.0 / (ROPE_BASE ** (jnp.arange(0, QK_ROPE, 2, dtype=jnp.float32) / QK_ROPE))
    ang = positions.astype(jnp.float32)[..., None] * inv_freq
    return jnp.cos(ang), jnp.sin(ang)


def apply_rope(x, cos, sin):
    xf = x.astype(jnp.float32)
    x1, x2 = jnp.split(xf, 2, axis=-1)
    return jnp.concatenate([x1 * cos - x2 * sin, x1 * sin + x2 * cos], axis=-1).astype(x.dtype)


def swiglu(h, w_gate_up, w_down):
    g, u = jnp.split(h @ w_gate_up, 2, axis=-1)
    return (jax.nn.silu(g) * u) @ w_down


def mla(h, cos, sin, w_down, q_norm, w_uq, kv_norm, w_ukv, w_o):
    B, S, _ = h.shape
    c_q, c_kv, k_rope = jnp.split(h @ w_down, [Q_LORA, Q_LORA + KV_LORA], axis=-1)
    q = (rms_norm(c_q, q_norm) @ w_uq).reshape(B, S, MLA_HEADS, QK_NOPE + QK_ROPE)
    q_nope = q[..., :QK_NOPE]
    q_rope = apply_rope(q[..., QK_NOPE:], cos[:, :, None, :], sin[:, :, None, :])
    k_rope = apply_rope(k_rope, cos, sin)
    kv = (rms_norm(c_kv, kv_norm) @ w_ukv).reshape(B, S, MLA_HEADS, QK_NOPE + V_HEAD)
    k_nope, v = kv[..., :QK_NOPE], kv[..., QK_NOPE:]
    scale = (QK_NOPE + QK_ROPE) ** -0.5
    outs = []
    for start in range(0, S, Q_BLOCK):
        end = start + Q_BLOCK
        s = (jnp.einsum('bqhd,bkhd->bhqk', q_nope[:, start:end], k_nope[:, :end])
             + jnp.einsum('bqhr,bkr->bhqk', q_rope[:, start:end], k_rope[:, :end]))
        s = s.astype(jnp.float32) * scale
        q_chunk = (start + jnp.arange(Q_BLOCK)) // CHUNK
        k_chunk = jnp.arange(end) // CHUNK
        mask = k_chunk[None, :] <= q_chunk[:, None]
        p = jax.nn.softmax(jnp.where(mask, s, -jnp.inf), axis=-1).astype(v.dtype)
        outs.append(jnp.einsum('bhqk,bkhd->bqhd', p, v[:, :end]))
    o = jnp.concatenate(outs, axis=1).reshape(B, S, MLA_HEADS * V_HEAD)
    return o @ w_o


def stick_breaking(h, w_q, w_o, k, v):
    B, S, _ = h.shape
    q = (h @ w_q).reshape(B, S, SB_HEADS, SB_HEAD)
    scale = SB_HEAD ** -0.5
    outs = []
    for start in range(0, S, Q_BLOCK):
        end = start + Q_BLOCK
        z = jnp.einsum('bqhd,bkhd->bhqk', q[:, start:end], k[:, :end]).astype(jnp.float32) * scale
        t = start + jnp.arange(Q_BLOCK)
        s = jnp.arange(end)
        mask = s[None, :] < t[:, None]
        log_1m_beta = jnp.where(mask, jax.nn.log_sigmoid(-z), 0.0)
        between = lax.cumsum(log_1m_beta, axis=3, reverse=True) - log_1m_beta
        log_a = jnp.where(mask, jax.nn.log_sigmoid(z) + between, -jnp.inf)
        a = jnp.exp(log_a).astype(v.dtype)
        outs.append(jnp.einsum('bhqk,bkhd->bqhd', a, v[:, :end]))
    o = jnp.concatenate(outs, axis=1).reshape(B, S, SB_HEADS * SB_HEAD)
    return o @ w_o


def moe(h, router, w_gate_up, w_down):
    B, S, D = h.shape
    xt = h.reshape(B * S, D)
    probs = jax.nn.softmax((xt @ router).astype(jnp.float32), axis=-1)
    top_p, top_i = lax.top_k(probs, TOP_K)
    top_p = top_p / jnp.sum(top_p, axis=-1, keepdims=True)
    gates = jnp.sum(jax.nn.one_hot(top_i, N_EXPERTS, dtype=jnp.float32) * top_p[..., None], axis=1)
    gates = gates.astype(xt.dtype)
    y = jnp.zeros_like(xt)
    for e in range(N_EXPERTS):
        y = y + gates[:, e:e + 1] * swiglu(xt, w_gate_up[e], w_down[e])
    return y.reshape(B, S, D)


def setup_inputs(seed: int = 0) -> dict:
    key = jax.random.key(seed)
    keys = iter(jax.random.split(key, 64))
    f32 = jnp.float32

    def w(shape, fan_in):
        return jax.random.normal(next(keys), shape, f32) * (fan_in ** -0.5)

    def g(n):
        return 1.0 + 0.01 * jax.random.normal(next(keys), (n,), f32)

    d = {}
    d["x"] = jax.random.normal(next(keys), (BATCH, SEQ, D_MODEL), f32)
    offsets = jax.random.randint(next(keys), (BATCH, 1), 0, 4096, dtype=jnp.int32)
    d["positions"] = offsets + jnp.arange(SEQ, dtype=jnp.int32)[None, :]

    def add_mla(i):
        d[f"attn_norm_{i}"] = g(D_MODEL)
        d[f"mla_w_down_{i}"] = w((D_MODEL, Q_LORA + KV_LORA + QK_ROPE), D_MODEL)
        d[f"mla_q_norm_{i}"] = g(Q_LORA)
        d[f"mla_w_uq_{i}"] = w((Q_LORA, MLA_HEADS * (QK_NOPE + QK_ROPE)), Q_LORA)
        d[f"mla_kv_norm_{i}"] = g(KV_LORA)
        d[f"mla_w_ukv_{i}"] = w((KV_LORA, MLA_HEADS * (QK_NOPE + V_HEAD)), KV_LORA)
        d[f"mla_w_o_{i}"] = w((MLA_HEADS * V_HEAD, D_MODEL), MLA_HEADS * V_HEAD)

    def add_sb(i):
        d[f"attn_norm_{i}"] = g(D_MODEL)
        d[f"sb_w_q_{i}"] = w((D_MODEL, SB_HEADS * SB_HEAD), D_MODEL)
        d[f"sb_w_o_{i}"] = w((SB_HEADS * SB_HEAD, D_MODEL), SB_HEADS * SB_HEAD)

    def add_dense(i):
        d[f"ffn_norm_{i}"] = g(D_MODEL)
        d[f"ffn_w_gate_up_{i}"] = w((D_MODEL, 2 * D_FF), D_MODEL)
        d[f"ffn_w_down_{i}"] = w((D_FF, D_MODEL), D_FF)

    def add_moe(i):
        d[f"ffn_norm_{i}"] = g(D_MODEL)
        d[f"moe_router_{i}"] = w((D_MODEL, N_EXPERTS), D_MODEL)
        d[f"moe_w_gate_up_{i}"] = w((N_EXPERTS, D_MODEL, 2 * D_FF_EXPERT), D_MODEL)
        d[f"moe_w_down_{i}"] = w((N_EXPERTS, D_FF_EXPERT, D_MODEL), D_FF_EXPERT)

    add_mla(0); add_dense(0)
    add_mla(1); add_moe(1)
    d["kv_shared_norm"] = g(D_MODEL)
    d["kv_shared_w"] = w((D_MODEL, 2 * SB_HEADS * SB_HEAD), D_MODEL)
    add_sb(2); add_dense(2)
    add_sb(3); add_moe(3)
    d["final_norm"] = g(D_MODEL)
    return d


def reference(x, positions,
              attn_norm_0, mla_w_down_0, mla_q_norm_0, mla_w_uq_0, mla_kv_norm_0, mla_w_ukv_0, mla_w_o_0,
              ffn_norm_0, ffn_w_gate_up_0, ffn_w_down_0,
              attn_norm_1, mla_w_down_1, mla_q_norm_1, mla_w_uq_1, mla_kv_norm_1, mla_w_ukv_1, mla_w_o_1,
              ffn_norm_1, moe_router_1, moe_w_gate_up_1, moe_w_down_1,
              kv_shared_norm, kv_shared_w,
              attn_norm_2, sb_w_q_2, sb_w_o_2,
              ffn_norm_2, ffn_w_gate_up_2, ffn_w_down_2,
              attn_norm_3, sb_w_q_3, sb_w_o_3,
              ffn_norm_3, moe_router_3, moe_w_gate_up_3, moe_w_down_3,
              final_norm):
    B, S, _ = x.shape
    attn_norms = [attn_norm_0, attn_norm_1, attn_norm_2, attn_norm_3]
    ffn_norms = [ffn_norm_0, ffn_norm_1, ffn_norm_2, ffn_norm_3]
    mla_params = [
        (mla_w_down_0, mla_q_norm_0, mla_w_uq_0, mla_kv_norm_0, mla_w_ukv_0, mla_w_o_0),
        (mla_w_down_1, mla_q_norm_1, mla_w_uq_1, mla_kv_norm_1, mla_w_ukv_1, mla_w_o_1),
    ]
    sb_params = [(sb_w_q_2, sb_w_o_2), (sb_w_q_3, sb_w_o_3)]
    dense_params = [(ffn_w_gate_up_0, ffn_w_down_0), (ffn_w_gate_up_2, ffn_w_down_2)]
    moe_params = [(moe_router_1, moe_w_gate_up_1, moe_w_down_1),
                  (moe_router_3, moe_w_gate_up_3, moe_w_down_3)]

    cos, sin = rope_cos_sin(positions)
    h = x
    k_sh = v_sh = None
    for layer in range(DEPTH):
        if layer < N_A_LAYERS:
            h = h + mla(rms_norm(h, attn_norms[layer]), cos, sin, *mla_params[layer])
        else:
            if layer == N_A_LAYERS:
                kv = (rms_norm(h, kv_shared_norm) @ kv_shared_w).reshape(B, S, 2, SB_HEADS, SB_HEAD)
                k_sh, v_sh = kv[:, :, 0], kv[:, :, 1]
            h = h + stick_breaking(rms_norm(h, attn_norms[layer]), *sb_params[layer - N_A_LAYERS], k_sh, v_sh)
        f_in = rms_norm(h, ffn_norms[layer])
        if layer % 2 == 0:
            h = h + swiglu(f_in, *dense_params[layer // 2])
        else:
            h = h + moe(f_in, *moe_params[layer // 2])
    return rms_norm(h, final_norm)
```

```python
import functools

import jax
import jax.numpy as jnp
from jax import lax
from jax.experimental import pallas as pl
from jax.experimental.pallas import tpu as pltpu

F32 = jnp.float32
BF16 = jnp.bfloat16

D_MODEL = 1024
CHUNK = 64
MLA_HEADS = 8
QK_NOPE = 128
QK_ROPE = 64
V_HEAD = 128
Q_LORA = 384
KV_LORA = 256
ROPE_BASE = 10000.0
SB_HEADS = 8
SB_HEAD = 128
D_FF = 2816
N_EXPERTS = 8
TOP_K = 2
EPS = 1e-6

LANES = 128
MLA_QK_PAD = 256
DOWN_PAD = 768
VMEM_LIMIT = 56 * 1024 * 1024

TM_PROJ = 512
TM_FFN = 512
TF_FFN = 1408
TM_MOE = 512
TQ_MLA = 512
TQ_SB = 512
TK_SB = 256
TM_ROUTER = 512


def _rms(x, g):
    ms = jnp.mean(x * x, axis=-1, keepdims=True)
    return x * lax.rsqrt(ms + EPS) * g


def _dot(a, b):
    return jnp.dot(a, b, preferred_element_type=F32)


def _dot_nt(a, b):
    return lax.dot_general(a, b, (((1,), (1,)), ((), ())), preferred_element_type=F32)


def _silu(g):
    return g / (1.0 + jnp.exp(-g))


def _params(*sem):
    return pltpu.CompilerParams(dimension_semantics=sem, vmem_limit_bytes=VMEM_LIMIT)


def _mla_proj_kernel(h_ref, g_ref, wd_ref, qn_ref, kvn_ref, wuq_ref, wuk_ref, wuv_ref,
                     c_ref, sa_ref, sb_ref, q_ref, k_ref, v_ref, *, scale):
    xn = _rms(h_ref[...], g_ref[...]).astype(BF16)
    c = _dot(xn, wd_ref[...])
    cq = _rms(c[:, :Q_LORA], qn_ref[...]).astype(BF16)
    ckv = _rms(c[:, Q_LORA:Q_LORA + KV_LORA], kvn_ref[...]).astype(BF16)
    cos_t, sin_a, sin_b = c_ref[...], sa_ref[...], sb_ref[...]

    def rope(r):
        return r * cos_t + pltpu.roll(r, 96, 1) * sin_a + pltpu.roll(r, 32, 1) * sin_b

    kr = rope(c[:, Q_LORA + KV_LORA:]).astype(BF16)
    q = _dot(cq, wuq_ref[...])
    kn = _dot(ckv, wuk_ref[...])
    v_ref[...] = _dot(ckv, wuv_ref[...]).astype(BF16)
    for h in range(MLA_HEADS):
        lo = h * MLA_QK_PAD
        q_ref[:, lo:lo + LANES] = (q[:, lo:lo + LANES] * scale).astype(BF16)
        q_ref[:, lo + LANES:lo + 2 * LANES] = (rope(q[:, lo + LANES:lo + 2 * LANES]) * scale).astype(BF16)
        k_ref[:, lo:lo + LANES] = kn[:, h * LANES:(h + 1) * LANES].astype(BF16)
        k_ref[:, lo + LANES:lo + 2 * LANES] = kr


def _mla_proj(h, g, wd, qn, kvn, wuq, wuk, wuv, cos_t, sin_a, sin_b):
    n = h.shape[0]
    tm = TM_PROJ
    row = lambda i: (i, 0)
    fixed = lambda i: (0, 0)
    scale = float((QK_NOPE + QK_ROPE) ** -0.5)
    return pl.pallas_call(
        functools.partial(_mla_proj_kernel, scale=scale),
        out_shape=(jax.ShapeDtypeStruct((n, MLA_HEADS * MLA_QK_PAD), BF16),
                   jax.ShapeDtypeStruct((n, MLA_HEADS * MLA_QK_PAD), BF16),
                   jax.ShapeDtypeStruct((n, MLA_HEADS * V_HEAD), BF16)),
        grid=(n // tm,),
        in_specs=[pl.BlockSpec((tm, D_MODEL), row),
                  pl.BlockSpec((1, D_MODEL), fixed),
                  pl.BlockSpec((D_MODEL, DOWN_PAD), fixed),
                  pl.BlockSpec((1, Q_LORA), fixed),
                  pl.BlockSpec((1, KV_LORA), fixed),
                  pl.BlockSpec((Q_LORA, MLA_HEADS * MLA_QK_PAD), fixed),
                  pl.BlockSpec((KV_LORA, MLA_HEADS * QK_NOPE), fixed),
                  pl.BlockSpec((KV_LORA, MLA_HEADS * V_HEAD), fixed),
                  pl.BlockSpec((tm, LANES), row),
                  pl.BlockSpec((tm, LANES), row),
                  pl.BlockSpec((tm, LANES), row)],
        out_specs=(pl.BlockSpec((tm, MLA_HEADS * MLA_QK_PAD), row),
                   pl.BlockSpec((tm, MLA_HEADS * MLA_QK_PAD), row),
                   pl.BlockSpec((tm, MLA_HEADS * V_HEAD), row)),
        compiler_params=_params("parallel"),
        name="mla_proj",
    )(h, g, wd, qn, kvn, wuq, wuk, wuv, cos_t, sin_a, sin_b)


def _mla_attn_kernel(q_ref, k_ref, v_ref, o_ref, *, tq):
    i = pl.program_id(2)
    q = q_ref[0]

    def step(j, carry, masked):
        m, l, acc = carry
        start = pl.multiple_of(j * tq, tq)
        kb = k_ref[0, pl.ds(start, tq), :]
        vb = v_ref[0, pl.ds(start, tq), :]
        s = _dot_nt(q, kb)
        if masked:
            qc = lax.broadcasted_iota(jnp.int32, (tq, tq), 0) // CHUNK
            kc = lax.broadcasted_iota(jnp.int32, (tq, tq), 1) // CHUNK
            s = jnp.where(kc <= qc, s, -jnp.inf)
        m_new = jnp.maximum(m, jnp.max(s, axis=-1, keepdims=True))
        alpha = jnp.exp(m - m_new)
        p = jnp.exp(s - m_new)
        l = alpha * l + jnp.sum(p, axis=-1, keepdims=True)
        acc = alpha * acc + _dot(p.astype(BF16), vb)
        return m_new, l, acc

    init = (jnp.full((tq, 1), -jnp.inf, F32), jnp.zeros((tq, 1), F32), jnp.zeros((tq, V_HEAD), F32))
    carry = lax.fori_loop(0, i, lambda j, c: step(j, c, False), init)
    _, l, acc = step(i, carry, True)
    o_ref[0] = (acc / l).astype(BF16)


def _mla_attn(q, k, v):
    b, s, _ = q.shape
    tq = TQ_MLA
    return pl.pallas_call(
        functools.partial(_mla_attn_kernel, tq=tq),
        out_shape=jax.ShapeDtypeStruct((b, s, MLA_HEADS * V_HEAD), BF16),
        grid=(b, MLA_HEADS, s // tq),
        in_specs=[pl.BlockSpec((1, tq, MLA_QK_PAD), lambda bi, h, i: (bi, i, h)),
                  pl.BlockSpec((1, s, MLA_QK_PAD), lambda bi, h, i: (bi, 0, h)),
                  pl.BlockSpec((1, s, V_HEAD), lambda bi, h, i: (bi, 0, h))],
        out_specs=pl.BlockSpec((1, tq, V_HEAD), lambda bi, h, i: (bi, i, h)),
        compiler_params=_params("parallel", "parallel", "arbitrary"),
        name="mla_attn",
    )(q, k, v)


def _proj_res_kernel(h_ref, a_ref, w_ref, o_ref):
    o_ref[...] = h_ref[...] + _dot(a_ref[...], w_ref[...])


def _proj_residual(h, a, w):
    n = h.shape[0]
    tm = TM_PROJ
    row = lambda i: (i, 0)
    return pl.pallas_call(
        _proj_res_kernel,
        out_shape=jax.ShapeDtypeStruct(h.shape, F32),
        grid=(n // tm,),
        in_specs=[pl.BlockSpec((tm, D_MODEL), row),
                  pl.BlockSpec((tm, a.shape[1]), row),
                  pl.BlockSpec(w.shape, lambda i: (0, 0))],
        out_specs=pl.BlockSpec((tm, D_MODEL), row),
        input_output_aliases={0: 0},
        compiler_params=_params("parallel"),
        name="proj_residual",
    )(h, a, w)


def _norm_matmul_kernel(h_ref, g_ref, w_ref, o_ref, *, scale):
    xn = _rms(h_ref[...], g_ref[...]).astype(BF16)
    o_ref[...] = (_dot(xn, w_ref[...]) * scale).astype(BF16)


def _norm_matmul(h, g, w, scale=1.0):
    n = h.shape[0]
    tm = TM_PROJ
    f = w.shape[1]
    row = lambda i: (i, 0)
    return pl.pallas_call(
        functools.partial(_norm_matmul_kernel, scale=float(scale)),
        out_shape=jax.ShapeDtypeStruct((n, f), BF16),
        grid=(n // tm,),
        in_specs=[pl.BlockSpec((tm, D_MODEL), row),
                  pl.BlockSpec((1, D_MODEL), lambda i: (0, 0)),
                  pl.BlockSpec(w.shape, lambda i: (0, 0))],
        out_specs=pl.BlockSpec((tm, f), row),
        compiler_params=_params("parallel"),
        name="norm_matmul",
    )(h, g, w)


def _ffn_kernel(h_ref, g_ref, wg_ref, wu_ref, wd_ref, o_ref, xn_ref, acc_ref):
    j = pl.program_id(1)

    @pl.when(j == 0)
    def _():
        xn_ref[...] = _rms(h_ref[...], g_ref[...]).astype(BF16)

    x = xn_ref[...]
    gate = _dot(x, wg_ref[...])
    up = _dot(x, wu_ref[...])
    part = _dot((_silu(gate) * up).astype(BF16), wd_ref[...])

    @pl.when(j == 0)
    def _():
        acc_ref[...] = part

    @pl.when(j > 0)
    def _():
        acc_ref[...] += part

    @pl.when(j == pl.num_programs(1) - 1)
    def _():
        o_ref[...] = h_ref[...] + acc_ref[...]


def _dense_ffn(h, g, w_gate_up, w_down):
    n = h.shape[0]
    tm, tf = TM_FFN, TF_FFN
    nj = D_FF // tf
    return pl.pallas_call(
        _ffn_kernel,
        out_shape=jax.ShapeDtypeStruct(h.shape, F32),
        grid=(n // tm, nj),
        in_specs=[pl.BlockSpec((tm, D_MODEL), lambda i, j: (i, 0)),
                  pl.BlockSpec((1, D_MODEL), lambda i, j: (0, 0)),
                  pl.BlockSpec((D_MODEL, tf), lambda i, j: (0, j)),
                  pl.BlockSpec((D_MODEL, tf), lambda i, j: (0, j + nj)),
                  pl.BlockSpec((tf, D_MODEL), lambda i, j: (j, 0))],
        out_specs=pl.BlockSpec((tm, D_MODEL), lambda i, j: (i, 0)),
        scratch_shapes=[pltpu.VMEM((tm, D_MODEL), BF16), pltpu.VMEM((tm, D_MODEL), F32)],
        input_output_aliases={0: 0},
        compiler_params=_params("parallel", "arbitrary"),
        name="dense_ffn",
    )(h, g, w_gate_up, w_gate_up, w_down)


def _router_kernel(h_ref, g_ref, rhi_ref, rlo_ref, xn_ref, idx_ref, gate_ref):
    xn = _rms(h_ref[...], g_ref[...])
    x_hi = xn.astype(BF16)
    x_lo = (xn - x_hi.astype(F32)).astype(BF16)
    xn_ref[...] = x_hi
    logits = _dot_nt(rhi_ref[...], x_hi) + _dot_nt(rhi_ref[...], x_lo) + _dot_nt(rlo_ref[...], x_hi)
    m = jnp.max(logits, axis=0, keepdims=True)
    e = jnp.exp(logits - m)
    p = e / jnp.sum(e, axis=0, keepdims=True)
    eid = lax.broadcasted_iota(jnp.int32, p.shape, 0)
    p1 = jnp.max(p, axis=0, keepdims=True)
    i1 = jnp.min(jnp.where(p == p1, eid, N_EXPERTS), axis=0, keepdims=True)
    pm = jnp.where(eid == i1, -1.0, p)
    p2 = jnp.max(pm, axis=0, keepdims=True)
    i2 = jnp.min(jnp.where(pm == p2, eid, N_EXPERTS), axis=0, keepdims=True)
    den = p1 + p2
    idx_ref[...] = jnp.concatenate([i1, i2], axis=0)
    gate_ref[...] = jnp.concatenate([p1 / den, p2 / den], axis=0)


def _router(h, g, r_hi, r_lo):
    n = h.shape[0]
    tm = TM_ROUTER
    return pl.pallas_call(
        _router_kernel,
        out_shape=(jax.ShapeDtypeStruct((n, D_MODEL), BF16),
                   jax.ShapeDtypeStruct((TOP_K, n), jnp.int32),
                   jax.ShapeDtypeStruct((TOP_K, n), F32)),
        grid=(n // tm,),
        in_specs=[pl.BlockSpec((tm, D_MODEL), lambda i: (i, 0)),
                  pl.BlockSpec((1, D_MODEL), lambda i: (0, 0)),
                  pl.BlockSpec((N_EXPERTS, D_MODEL), lambda i: (0, 0)),
                  pl.BlockSpec((N_EXPERTS, D_MODEL), lambda i: (0, 0))],
        out_specs=(pl.BlockSpec((tm, D_MODEL), lambda i: (i, 0)),
                   pl.BlockSpec((TOP_K, tm), lambda i: (0, i)),
                   pl.BlockSpec((TOP_K, tm), lambda i: (0, i))),
        compiler_params=_params("parallel"),
        name="moe_router",
    )(h, g, r_hi, r_lo)


def _moe_kernel(te_ref, nu_ref, x_ref, wg_ref, wu_ref, wd_ref, gate_ref, o_ref, acc_ref):
    t = pl.program_id(0)
    j = pl.program_id(1)
    last = pl.num_programs(1) - 1
    used = t < nu_ref[0]

    @pl.when(used)
    def _():
        x = x_ref[...]
        gate = _dot(x, wg_ref[0])
        up = _dot(x, wu_ref[0])
        part = _dot((_silu(gate) * up).astype(BF16), wd_ref[0])

        @pl.when(j == 0)
        def _():
            acc_ref[...] = part

        @pl.when(j > 0)
        def _():
            acc_ref[...] += part

        @pl.when(j == last)
        def _():
            o_ref[...] = (acc_ref[...] * gate_ref[...]).astype(BF16)

    @pl.when(jnp.logical_and(jnp.logical_not(used), j == last))
    def _():
        o_ref[...] = jnp.zeros(o_ref.shape, BF16)


def _moe_experts(tile_expert, n_used, x_sorted, w_gate_up, w_down, gate_sorted):
    p = x_sorted.shape[0]
    tm, tf = TM_MOE, TF_FFN
    nj = D_FF // tf
    grid_spec = pltpu.PrefetchScalarGridSpec(
        num_scalar_prefetch=2,
        grid=(p // tm, nj),
        in_specs=[pl.BlockSpec((tm, D_MODEL), lambda t, j, te, nu: (t, 0)),
                  pl.BlockSpec((1, D_MODEL, tf), lambda t, j, te, nu: (te[t], 0, j)),
                  pl.BlockSpec((1, D_MODEL, tf), lambda t, j, te, nu: (te[t], 0, j + nj)),
                  pl.BlockSpec((1, tf, D_MODEL), lambda t, j, te, nu: (te[t], j, 0)),
                  pl.BlockSpec((tm, 1), lambda t, j, te, nu: (t, 0))],
        out_specs=pl.BlockSpec((tm, D_MODEL), lambda t, j, te, nu: (t, 0)),
        scratch_shapes=[pltpu.VMEM((tm, D_MODEL), F32)],
    )
    return pl.pallas_call(
        _moe_kernel,
        out_shape=jax.ShapeDtypeStruct((p, D_MODEL), BF16),
        grid_spec=grid_spec,
        compiler_params=_params("arbitrary", "arbitrary"),
        name="moe_experts",
    )(tile_expert, n_used, x_sorted, w_gate_up, w_gate_up, w_down, gate_sorted)


def _moe_layer(h, g, router, w_gate_up, w_down):
    n = h.shape[0]
    tm = TM_MOE
    r_t = router.T
    r_hi = r_t.astype(BF16)
    r_lo = (r_t - r_hi.astype(F32)).astype(BF16)
    xn, idx, gates = _router(h, g, r_hi, r_lo)

    e_flat = idx.reshape(-1)
    onehot = (e_flat[:, None] == jnp.arange(N_EXPERTS, dtype=jnp.int32)[None, :]).astype(jnp.int32)
    csum = jnp.cumsum(onehot, axis=0)
    rank = jnp.sum(csum * onehot, axis=1) - 1
    counts = csum[-1]
    padded = ((counts + tm - 1) // tm) * tm
    ends = jnp.cumsum(padded)
    offsets = ends - padded
    pos = offsets[e_flat] + rank
    p_rows = TOP_K * n + N_EXPERTS * tm
    n_tiles = p_rows // tm
    n_used = (ends[-1] // tm).astype(jnp.int32)
    tile_start = jnp.arange(n_tiles, dtype=jnp.int32) * tm
    tile_expert = jnp.minimum(jnp.searchsorted(ends, tile_start, side="right"), N_EXPERTS - 1).astype(jnp.int32)
    last_expert = tile_expert[jnp.maximum(n_used - 1, 0)]
    tile_expert = jnp.where(jnp.arange(n_tiles) < n_used, tile_expert, last_expert)
    tok = jnp.tile(jnp.arange(n, dtype=jnp.int32), TOP_K)
    tok_sorted = jnp.zeros((p_rows,), jnp.int32).at[pos].set(tok)
    gate_sorted = jnp.zeros((p_rows,), F32).at[pos].set(gates.reshape(-1))
    x_sorted = jnp.take(xn, tok_sorted, axis=0)

    y_sorted = _moe_experts(tile_expert, n_used.reshape(1), x_sorted, w_gate_up, w_down,
                            gate_sorted.reshape(p_rows, 1))
    y = jnp.take(y_sorted, pos[:n], axis=0).astype(F32) + jnp.take(y_sorted, pos[n:], axis=0).astype(F32)
    return h + y


def _sb_attn_kernel(q_ref, k_ref, v_ref, u_ref, o_ref, *, tq, tk):
    i = pl.program_id(2)
    q = q_ref[0]
    tri = u_ref[...]
    ratio = tq // tk

    def block(jb, carry, masked):
        r_sum, acc = carry
        start = pl.multiple_of(jb * tk, tk)
        kb = k_ref[0, pl.ds(start, tk), :]
        vb = v_ref[0, pl.ds(start, tk), :]
        z = _dot_nt(q, kb)
        soft = jnp.log(1.0 + jnp.exp(-jnp.abs(z)))
        ls = jnp.minimum(z, 0.0) - soft
        lb = ls - z
        if masked:
            row = lax.broadcasted_iota(jnp.int32, (tq, tk), 0) + i * tq
            col = lax.broadcasted_iota(jnp.int32, (tq, tk), 1) + jb * tk
            valid = col < row
            lb = jnp.where(valid, lb, 0.0)
        between = _dot(lb.astype(BF16), tri)
        a = jnp.exp(ls + between + r_sum)
        if masked:
            a = jnp.where(valid, a, 0.0)
        acc = acc + _dot(a.astype(BF16), vb)
        r_sum = r_sum + jnp.sum(lb, axis=-1, keepdims=True)
        return r_sum, acc

    carry = (jnp.zeros((tq, 1), F32), jnp.zeros((tq, SB_HEAD), F32))
    for d in range(ratio):
        carry = block(i * ratio + (ratio - 1 - d), carry, True)
    n_full = i * ratio
    carry = lax.fori_loop(0, n_full, lambda jj, c: block(n_full - 1 - jj, c, False), carry)
    o_ref[0] = carry[1].astype(BF16)


def _sb_attn(q, kv, tri):
    b, s, _ = q.shape
    tq, tk = TQ_SB, TK_SB
    return pl.pallas_call(
        functools.partial(_sb_attn_kernel, tq=tq, tk=tk),
        out_shape=jax.ShapeDtypeStruct((b, s, SB_HEADS * SB_HEAD), BF16),
        grid=(b, SB_HEADS, s // tq),
        in_specs=[pl.BlockSpec((1, tq, SB_HEAD), lambda bi, h, i: (bi, i, h)),
                  pl.BlockSpec((1, s, SB_HEAD), lambda bi, h, i: (bi, 0, h)),
                  pl.BlockSpec((1, s, SB_HEAD), lambda bi, h, i: (bi, 0, SB_HEADS + h)),
                  pl.BlockSpec((tk, tk), lambda bi, h, i: (0, 0))],
        out_specs=pl.BlockSpec((1, tq, SB_HEAD), lambda bi, h, i: (bi, i, h)),
        compiler_params=_params("parallel", "parallel", "arbitrary"),
        name="sb_attn",
    )(q, kv, kv, tri)


def _final_norm_kernel(h_ref, g_ref, o_ref):
    o_ref[...] = _rms(h_ref[...], g_ref[...])


def _final_norm(h, g):
    n = h.shape[0]
    tm = TM_PROJ
    return pl.pallas_call(
        _final_norm_kernel,
        out_shape=jax.ShapeDtypeStruct(h.shape, F32),
        grid=(n // tm,),
        in_specs=[pl.BlockSpec((tm, D_MODEL), lambda i: (i, 0)),
                  pl.BlockSpec((1, D_MODEL), lambda i: (0, 0))],
        out_specs=pl.BlockSpec((tm, D_MODEL), lambda i: (i, 0)),
        compiler_params=_params("parallel"),
        name="final_norm",
    )(h, g)


def _rope_tables(positions):
    inv_freq = 1.0 / (ROPE_BASE ** (jnp.arange(0, QK_ROPE, 2, dtype=F32) / QK_ROPE))
    ang = positions.astype(F32).reshape(-1, 1) * inv_freq
    cos, sin = jnp.cos(ang), jnp.sin(ang)
    z32 = jnp.zeros_like(cos)
    z64 = jnp.zeros((cos.shape[0], LANES - QK_ROPE), F32)
    cos_t = jnp.concatenate([cos, cos, z64], axis=-1)
    sin_a = jnp.concatenate([-sin, z32, z64], axis=-1)
    sin_b = jnp.concatenate([z32, sin, z64], axis=-1)
    return cos_t, sin_a, sin_b


def _mla_weights(w_down, w_uq, w_ukv):
    wd = jnp.pad(w_down, ((0, 0), (0, DOWN_PAD - w_down.shape[1]))).astype(BF16)
    wuq = w_uq.reshape(Q_LORA, MLA_HEADS, QK_NOPE + QK_ROPE)
    wuq = jnp.pad(wuq, ((0, 0), (0, 0), (0, MLA_QK_PAD - QK_NOPE - QK_ROPE)))
    wuq = wuq.reshape(Q_LORA, MLA_HEADS * MLA_QK_PAD).astype(BF16)
    wukv = w_ukv.reshape(KV_LORA, MLA_HEADS, QK_NOPE + V_HEAD)
    wuk = wukv[:, :, :QK_NOPE].reshape(KV_LORA, MLA_HEADS * QK_NOPE).astype(BF16)
    wuv = wukv[:, :, QK_NOPE:].reshape(KV_LORA, MLA_HEADS * V_HEAD).astype(BF16)
    return wd, wuq, wuk, wuv


def kernel(x, positions, attn_norm_0, mla_w_down_0, mla_q_norm_0, mla_w_uq_0, mla_kv_norm_0, mla_w_ukv_0, mla_w_o_0, ffn_norm_0, ffn_w_gate_up_0, ffn_w_down_0, attn_norm_1, mla_w_down_1, mla_q_norm_1, mla_w_uq_1, mla_kv_norm_1, mla_w_ukv_1, mla_w_o_1, ffn_norm_1, moe_router_1, moe_w_gate_up_1, moe_w_down_1, kv_shared_norm, kv_shared_w, attn_norm_2, sb_w_q_2, sb_w_o_2, ffn_norm_2, ffn_w_gate_up_2, ffn_w_down_2, attn_norm_3, sb_w_q_3, sb_w_o_3, ffn_norm_3, moe_router_3, moe_w_gate_up_3, moe_w_down_3, final_norm):
    b, s, d = x.shape
    n = b * s
    row = lambda g: g.reshape(1, -1)
    h = x.reshape(n, d)
    cos_t, sin_a, sin_b = _rope_tables(positions)

    mla_layers = [
        (attn_norm_0, mla_w_down_0, mla_q_norm_0, mla_w_uq_0, mla_kv_norm_0, mla_w_ukv_0, mla_w_o_0),
        (attn_norm_1, mla_w_down_1, mla_q_norm_1, mla_w_uq_1, mla_kv_norm_1, mla_w_ukv_1, mla_w_o_1),
    ]
    sb_layers = [(attn_norm_2, sb_w_q_2, sb_w_o_2), (attn_norm_3, sb_w_q_3, sb_w_o_3)]
    dense = {0: (ffn_norm_0, ffn_w_gate_up_0, ffn_w_down_0), 2: (ffn_norm_2, ffn_w_gate_up_2, ffn_w_down_2)}
    moe = {1: (ffn_norm_1, moe_router_1, moe_w_gate_up_1, moe_w_down_1),
           3: (ffn_norm_3, moe_router_3, moe_w_gate_up_3, moe_w_down_3)}
    tri = (lax.broadcasted_iota(jnp.int32, (TK_SB, TK_SB), 0)
           > lax.broadcasted_iota(jnp.int32, (TK_SB, TK_SB), 1)).astype(BF16)

    kv_shared = None
    for layer in range(4):
        if layer < 2:
            an, w_down, qn, w_uq, kvn, w_ukv, w_o = mla_layers[layer]
            wd, wuq, wuk, wuv = _mla_weights(w_down, w_uq, w_ukv)
            q, k, v = _mla_proj(h, row(an), wd, row(qn), row(kvn), wuq, wuk, wuv, cos_t, sin_a, sin_b)
            o = _mla_attn(q.reshape(b, s, -1), k.reshape(b, s, -1), v.reshape(b, s, -1))
            h = _proj_residual(h, o.reshape(n, -1), w_o.astype(BF16))
        else:
            if layer == 2:
                kv_shared = _norm_matmul(h, row(kv_shared_norm), kv_shared_w.astype(BF16)).reshape(b, s, -1)
            an, w_q, w_o = sb_layers[layer - 2]
            q = _norm_matmul(h, row(an), w_q.astype(BF16), scale=SB_HEAD ** -0.5)
            o = _sb_attn(q.reshape(b, s, -1), kv_shared, tri)
            h = _proj_residual(h, o.reshape(n, -1), w_o.astype(BF16))
        if layer % 2 == 0:
            fn, w_gu, w_dn = dense[layer]
            h = _dense_ffn(h, row(fn), w_gu.astype(BF16), w_dn.astype(BF16))
        else:
            fn, router, w_gu, w_dn = moe[layer]
            h = _moe_layer(h, row(fn), router, w_gu.astype(BF16), w_dn.astype(BF16))
    return _final_norm(h, row(final_norm)).reshape(b, s, d)
```

```python
import functools

import jax
import jax.numpy as jnp
from jax import lax
from jax.experimental import pallas as pl
from jax.experimental.pallas import tpu as pltpu

F32 = jnp.float32
BF16 = jnp.bfloat16

D_MODEL = 1024
CHUNK = 64
MLA_HEADS = 8
QK_NOPE = 128
QK_ROPE = 64
V_HEAD = 128
Q_LORA = 384
KV_LORA = 256
ROPE_BASE = 10000.0
SB_HEADS = 8
SB_HEAD = 128
D_FF = 2816
N_EXPERTS = 8
TOP_K = 2
EPS = 1e-6

LANES = 128
MLA_QK_PAD = 256
DOWN_PAD = 768
VMEM_LIMIT = 56 * 1024 * 1024

TM_PROJ = 512
TM_FFN = 512
TF_FFN = 1408
TM_MOE = 512
TQ_MLA = 512
TK_MLA = 512
MLA_ROW_CHUNK = 32
MLA_HEADS_PER_STEP = 4
TB_SB = 256
SB_HEADS_PER_STEP = 4
TM_ROUTER = 512
LOG2E = 1.4426950408889634
SB_EXIT_LOG2 = -150.0


def _rms(x, g):
    ms = jnp.mean(x * x, axis=-1, keepdims=True)
    return x * lax.rsqrt(ms + EPS) * g


def _dot(a, b):
    return jnp.dot(a, b, preferred_element_type=F32)


def _dot_nt(a, b):
    return lax.dot_general(a, b, (((1,), (1,)), ((), ())), preferred_element_type=F32)


def _silu(g):
    return g / (1.0 + jnp.exp(-g))


def _aligned(x, m):
    return x if isinstance(x, int) else pl.multiple_of(x, m)


def _params(*sem):
    return pltpu.CompilerParams(dimension_semantics=sem, vmem_limit_bytes=VMEM_LIMIT)


def _mla_proj_kernel(h_ref, g_ref, wd_ref, qn_ref, kvn_ref, wuq_ref, wuk_ref, wuv_ref,
                     c_ref, sa_ref, sb_ref, q_ref, k_ref, v_ref, *, scale):
    xn = _rms(h_ref[...], g_ref[...]).astype(BF16)
    c = _dot(xn, wd_ref[...])
    cq = _rms(c[:, :Q_LORA], qn_ref[...]).astype(BF16)
    ckv = _rms(c[:, Q_LORA:Q_LORA + KV_LORA], kvn_ref[...]).astype(BF16)
    cos_t, sin_a, sin_b = c_ref[...], sa_ref[...], sb_ref[...]

    def rope(r):
        return r * cos_t + pltpu.roll(r, 96, 1) * sin_a + pltpu.roll(r, 32, 1) * sin_b

    kr = rope(c[:, Q_LORA + KV_LORA:]).astype(BF16)
    q = _dot(cq, wuq_ref[...])
    kn = _dot(ckv, wuk_ref[...])
    v_ref[...] = _dot(ckv, wuv_ref[...]).astype(BF16)
    for h in range(MLA_HEADS):
        lo = h * MLA_QK_PAD
        q_ref[:, lo:lo + LANES] = (q[:, lo:lo + LANES] * scale).astype(BF16)
        q_ref[:, lo + LANES:lo + 2 * LANES] = (rope(q[:, lo + LANES:lo + 2 * LANES]) * scale).astype(BF16)
        k_ref[:, lo:lo + LANES] = kn[:, h * LANES:(h + 1) * LANES].astype(BF16)
        k_ref[:, lo + LANES:lo + 2 * LANES] = kr


def _mla_proj(h, g, wd, qn, kvn, wuq, wuk, wuv, cos_t, sin_a, sin_b):
    n = h.shape[0]
    tm = TM_PROJ
    row = lambda i: (i, 0)
    fixed = lambda i: (0, 0)
    scale = float((QK_NOPE + QK_ROPE) ** -0.5 * LOG2E)
    return pl.pallas_call(
        functools.partial(_mla_proj_kernel, scale=scale),
        out_shape=(jax.ShapeDtypeStruct((n, MLA_HEADS * MLA_QK_PAD), BF16),
                   jax.ShapeDtypeStruct((n, MLA_HEADS * MLA_QK_PAD), BF16),
                   jax.ShapeDtypeStruct((n, MLA_HEADS * V_HEAD), BF16)),
        grid=(n // tm,),
        in_specs=[pl.BlockSpec((tm, D_MODEL), row),
                  pl.BlockSpec((1, D_MODEL), fixed),
                  pl.BlockSpec((D_MODEL, DOWN_PAD), fixed),
                  pl.BlockSpec((1, Q_LORA), fixed),
                  pl.BlockSpec((1, KV_LORA), fixed),
                  pl.BlockSpec((Q_LORA, MLA_HEADS * MLA_QK_PAD), fixed),
                  pl.BlockSpec((KV_LORA, MLA_HEADS * QK_NOPE), fixed),
                  pl.BlockSpec((KV_LORA, MLA_HEADS * V_HEAD), fixed),
                  pl.BlockSpec((tm, LANES), row),
                  pl.BlockSpec((tm, LANES), row),
                  pl.BlockSpec((tm, LANES), row)],
        out_specs=(pl.BlockSpec((tm, MLA_HEADS * MLA_QK_PAD), row),
                   pl.BlockSpec((tm, MLA_HEADS * MLA_QK_PAD), row),
                   pl.BlockSpec((tm, MLA_HEADS * V_HEAD), row)),
        compiler_params=_params("parallel"),
        name="mla_proj",
    )(h, g, wd, qn, kvn, wuq, wuk, wuv, cos_t, sin_a, sin_b)


def _mla_attn_kernel(q_ref, k_ref, v_ref, o_ref, s_ref, p_ref, m_ref, l_ref, alpha_ref, acc_ref, *, tq, tk, heads):
    i = pl.program_id(2)
    qg = [q_ref[0, :, g * MLA_QK_PAD:(g + 1) * MLA_QK_PAD] for g in range(heads)]
    rc = MLA_ROW_CHUNK

    m_ref[...] = jnp.full(m_ref.shape, -jnp.inf, F32)
    l_ref[...] = jnp.zeros(l_ref.shape, F32)
    acc_ref[...] = jnp.zeros(acc_ref.shape, F32)

    def scores(g, j):
        kb = k_ref[0, pl.ds(_aligned(j * tk, tk), tk), g * MLA_QK_PAD:(g + 1) * MLA_QK_PAD]
        s_ref[g] = _dot_nt(qg[g], kb)

    def update(g, j, diagonal):
        vb = v_ref[0, pl.ds(_aligned(j * tk, tk), tk), g * V_HEAD:(g + 1) * V_HEAD]
        for c in range(tq // rc):
            rows = slice(c * rc, (c + 1) * rc)
            s = s_ref[g, rows, :]
            if diagonal:
                qc = (lax.broadcasted_iota(jnp.int32, (rc, tk), 0) + c * rc) // CHUNK
                kc = lax.broadcasted_iota(jnp.int32, (rc, tk), 1) // CHUNK
                s = jnp.where(kc <= qc, s, -jnp.inf)
            m_old = m_ref[g, rows, :]
            m_new = jnp.maximum(m_old, jnp.max(s, axis=-1, keepdims=True))
            alpha = jnp.exp2(m_old - m_new)
            m_ref[g, rows, :] = m_new
            alpha_ref[g, rows, :] = alpha
            psum = jnp.zeros((rc, LANES), F32)
            for t in range(tk // LANES):
                p = jnp.exp2(s[:, t * LANES:(t + 1) * LANES] - m_new)
                psum = psum + p
                p_ref[g, rows, t * LANES:(t + 1) * LANES] = p.astype(BF16)
            l_ref[g, rows, :] = alpha * l_ref[g, rows, :] + jnp.sum(psum, axis=-1, keepdims=True)
        acc_ref[g] = alpha_ref[g] * acc_ref[g] + _dot(p_ref[g], vb)

    for g in range(heads):
        scores(g, 0)

    def body(j, carry):
        for g in range(heads):
            update(g, j, False)
            scores(g, j + 1)
        return carry

    lax.fori_loop(0, i, body, 0)
    for g in range(heads):
        update(g, i, True)
    for g in range(heads):
        o_ref[0, :, g * V_HEAD:(g + 1) * V_HEAD] = (acc_ref[g] / l_ref[g]).astype(BF16)


def _mla_attn(q, k, v):
    b, s, _ = q.shape
    tq, tk, hg = TQ_MLA, TK_MLA, MLA_HEADS_PER_STEP
    return pl.pallas_call(
        functools.partial(_mla_attn_kernel, tq=tq, tk=tk, heads=hg),
        out_shape=jax.ShapeDtypeStruct((b, s, MLA_HEADS * V_HEAD), BF16),
        grid=(b, MLA_HEADS // hg, s // tq),
        in_specs=[pl.BlockSpec((1, tq, hg * MLA_QK_PAD), lambda bi, h, i: (bi, i, h)),
                  pl.BlockSpec((1, s, hg * MLA_QK_PAD), lambda bi, h, i: (bi, 0, h)),
                  pl.BlockSpec((1, s, hg * V_HEAD), lambda bi, h, i: (bi, 0, h))],
        out_specs=pl.BlockSpec((1, tq, hg * V_HEAD), lambda bi, h, i: (bi, i, h)),
        scratch_shapes=[pltpu.VMEM((hg, tq, tk), F32), pltpu.VMEM((hg, tq, tk), BF16),
                        pltpu.VMEM((hg, tq, LANES), F32), pltpu.VMEM((hg, tq, LANES), F32),
                        pltpu.VMEM((hg, tq, LANES), F32), pltpu.VMEM((hg, tq, V_HEAD), F32)],
        compiler_params=_params("parallel", "parallel", "arbitrary"),
        name="mla_attn",
    )(q, k, v)


def _proj_res_kernel(h_ref, a_ref, w_ref, o_ref):
    o_ref[...] = h_ref[...] + _dot(a_ref[...], w_ref[...])


def _proj_residual(h, a, w):
    n = h.shape[0]
    tm = TM_PROJ
    row = lambda i: (i, 0)
    return pl.pallas_call(
        _proj_res_kernel,
        out_shape=jax.ShapeDtypeStruct(h.shape, F32),
        grid=(n // tm,),
        in_specs=[pl.BlockSpec((tm, D_MODEL), row),
                  pl.BlockSpec((tm, a.shape[1]), row),
                  pl.BlockSpec(w.shape, lambda i: (0, 0))],
        out_specs=pl.BlockSpec((tm, D_MODEL), row),
        input_output_aliases={0: 0},
        compiler_params=_params("parallel"),
        name="proj_residual",
    )(h, a, w)


def _norm_matmul_kernel(h_ref, g_ref, w_ref, o_ref, *, scale):
    xn = _rms(h_ref[...], g_ref[...]).astype(BF16)
    o_ref[...] = (_dot(xn, w_ref[...]) * scale).astype(BF16)


def _norm_matmul(h, g, w, scale=1.0):
    n = h.shape[0]
    tm = TM_PROJ
    f = w.shape[1]
    row = lambda i: (i, 0)
    return pl.pallas_call(
        functools.partial(_norm_matmul_kernel, scale=float(scale)),
        out_shape=jax.ShapeDtypeStruct((n, f), BF16),
        grid=(n // tm,),
        in_specs=[pl.BlockSpec((tm, D_MODEL), row),
                  pl.BlockSpec((1, D_MODEL), lambda i: (0, 0)),
                  pl.BlockSpec(w.shape, lambda i: (0, 0))],
        out_specs=pl.BlockSpec((tm, f), row),
        compiler_params=_params("parallel"),
        name="norm_matmul",
    )(h, g, w)


def _ffn_kernel(h_ref, g_ref, wg_ref, wu_ref, wd_ref, o_ref, xn_ref, acc_ref):
    j = pl.program_id(1)

    @pl.when(j == 0)
    def _():
        xn_ref[...] = _rms(h_ref[...], g_ref[...]).astype(BF16)

    x = xn_ref[...]
    gate = _dot(x, wg_ref[...])
    up = _dot(x, wu_ref[...])
    part = _dot((_silu(gate) * up).astype(BF16), wd_ref[...])

    @pl.when(j == 0)
    def _():
        acc_ref[...] = part

    @pl.when(j > 0)
    def _():
        acc_ref[...] += part

    @pl.when(j == pl.num_programs(1) - 1)
    def _():
        o_ref[...] = h_ref[...] + acc_ref[...]


def _dense_ffn(h, g, w_gate_up, w_down):
    n = h.shape[0]
    tm, tf = TM_FFN, TF_FFN
    nj = D_FF // tf
    return pl.pallas_call(
        _ffn_kernel,
        out_shape=jax.ShapeDtypeStruct(h.shape, F32),
        grid=(n // tm, nj),
        in_specs=[pl.BlockSpec((tm, D_MODEL), lambda i, j: (i, 0)),
                  pl.BlockSpec((1, D_MODEL), lambda i, j: (0, 0)),
                  pl.BlockSpec((D_MODEL, tf), lambda i, j: (0, j)),
                  pl.BlockSpec((D_MODEL, tf), lambda i, j: (0, j + nj)),
                  pl.BlockSpec((tf, D_MODEL), lambda i, j: (j, 0))],
        out_specs=pl.BlockSpec((tm, D_MODEL), lambda i, j: (i, 0)),
        scratch_shapes=[pltpu.VMEM((tm, D_MODEL), BF16), pltpu.VMEM((tm, D_MODEL), F32)],
        input_output_aliases={0: 0},
        compiler_params=_params("parallel", "arbitrary"),
        name="dense_ffn",
    )(h, g, w_gate_up, w_gate_up, w_down)


def _router_kernel(h_ref, g_ref, rhi_ref, rlo_ref, xn_ref, idx_ref, gate_ref):
    xn = _rms(h_ref[...], g_ref[...])
    x_hi = xn.astype(BF16)
    x_lo = (xn - x_hi.astype(F32)).astype(BF16)
    xn_ref[...] = x_hi
    logits = _dot_nt(rhi_ref[...], x_hi) + _dot_nt(rhi_ref[...], x_lo) + _dot_nt(rlo_ref[...], x_hi)
    m = jnp.max(logits, axis=0, keepdims=True)
    e = jnp.exp(logits - m)
    p = e / jnp.sum(e, axis=0, keepdims=True)
    eid = lax.broadcasted_iota(jnp.int32, p.shape, 0)
    p1 = jnp.max(p, axis=0, keepdims=True)
    i1 = jnp.min(jnp.where(p == p1, eid, N_EXPERTS), axis=0, keepdims=True)
    pm = jnp.where(eid == i1, -1.0, p)
    p2 = jnp.max(pm, axis=0, keepdims=True)
    i2 = jnp.min(jnp.where(pm == p2, eid, N_EXPERTS), axis=0, keepdims=True)
    den = p1 + p2
    idx_ref[...] = jnp.concatenate([i1, i2], axis=0)
    gate_ref[...] = jnp.concatenate([p1 / den, p2 / den], axis=0)


def _router(h, g, r_hi, r_lo):
    n = h.shape[0]
    tm = TM_ROUTER
    return pl.pallas_call(
        _router_kernel,
        out_shape=(jax.ShapeDtypeStruct((n, D_MODEL), BF16),
                   jax.ShapeDtypeStruct((TOP_K, n), jnp.int32),
                   jax.ShapeDtypeStruct((TOP_K, n), F32)),
        grid=(n // tm,),
        in_specs=[pl.BlockSpec((tm, D_MODEL), lambda i: (i, 0)),
                  pl.BlockSpec((1, D_MODEL), lambda i: (0, 0)),
                  pl.BlockSpec((N_EXPERTS, D_MODEL), lambda i: (0, 0)),
                  pl.BlockSpec((N_EXPERTS, D_MODEL), lambda i: (0, 0))],
        out_specs=(pl.BlockSpec((tm, D_MODEL), lambda i: (i, 0)),
                   pl.BlockSpec((TOP_K, tm), lambda i: (0, i)),
                   pl.BlockSpec((TOP_K, tm), lambda i: (0, i))),
        compiler_params=_params("parallel"),
        name="moe_router",
    )(h, g, r_hi, r_lo)


def _moe_kernel(te_ref, nu_ref, x_ref, wg_ref, wu_ref, wd_ref, o_ref, acc_ref):
    t = pl.program_id(0)
    j = pl.program_id(1)
    last = pl.num_programs(1) - 1
    used = t < nu_ref[0]

    @pl.when(used)
    def _():
        x = x_ref[...]
        gate = _dot(x, wg_ref[0])
        up = _dot(x, wu_ref[0])
        part = _dot((_silu(gate) * up).astype(BF16), wd_ref[0])

        @pl.when(j == 0)
        def _():
            acc_ref[...] = part

        @pl.when(j > 0)
        def _():
            acc_ref[...] += part

        @pl.when(j == last)
        def _():
            o_ref[...] = acc_ref[...].astype(BF16)

    @pl.when(jnp.logical_and(jnp.logical_not(used), j == last))
    def _():
        o_ref[...] = jnp.zeros(o_ref.shape, BF16)


def _moe_experts(tile_expert, n_used, x_sorted, w_gate_up, w_down):
    p = x_sorted.shape[0]
    tm, tf = TM_MOE, TF_FFN
    nj = D_FF // tf
    grid_spec = pltpu.PrefetchScalarGridSpec(
        num_scalar_prefetch=2,
        grid=(p // tm, nj),
        in_specs=[pl.BlockSpec((tm, D_MODEL), lambda t, j, te, nu: (t, 0)),
                  pl.BlockSpec((1, D_MODEL, tf), lambda t, j, te, nu: (te[t], 0, j)),
                  pl.BlockSpec((1, D_MODEL, tf), lambda t, j, te, nu: (te[t], 0, j + nj)),
                  pl.BlockSpec((1, tf, D_MODEL), lambda t, j, te, nu: (te[t], j, 0))],
        out_specs=pl.BlockSpec((tm, D_MODEL), lambda t, j, te, nu: (t, 0)),
        scratch_shapes=[pltpu.VMEM((tm, D_MODEL), F32)],
    )
    return pl.pallas_call(
        _moe_kernel,
        out_shape=jax.ShapeDtypeStruct((p, D_MODEL), BF16),
        grid_spec=grid_spec,
        compiler_params=_params("arbitrary", "arbitrary"),
        name="moe_experts",
    )(tile_expert, n_used, x_sorted, w_gate_up, w_gate_up, w_down)


def _moe_layer(h, g, router, w_gate_up, w_down):
    n = h.shape[0]
    tm = TM_MOE
    r_t = router.T
    r_hi = r_t.astype(BF16)
    r_lo = (r_t - r_hi.astype(F32)).astype(BF16)
    xn, idx, gates = _router(h, g, r_hi, r_lo)

    e_flat = idx.reshape(-1)
    onehot = (e_flat[:, None] == jnp.arange(N_EXPERTS, dtype=jnp.int32)[None, :]).astype(jnp.int32)
    csum = jnp.cumsum(onehot, axis=0)
    rank = jnp.sum(csum * onehot, axis=1) - 1
    counts = csum[-1]
    padded = ((counts + tm - 1) // tm) * tm
    ends = jnp.cumsum(padded)
    offsets = ends - padded
    pos = offsets[e_flat] + rank
    p_rows = TOP_K * n + N_EXPERTS * tm
    n_tiles = p_rows // tm
    n_used = (ends[-1] // tm).astype(jnp.int32)
    tile_start = jnp.arange(n_tiles, dtype=jnp.int32) * tm
    tile_expert = jnp.sum((tile_start[:, None] >= ends[None, :]).astype(jnp.int32), axis=1)
    tile_expert = jnp.minimum(tile_expert, N_EXPERTS - 1)
    last_expert = tile_expert[jnp.maximum(n_used - 1, 0)]
    tile_expert = jnp.where(jnp.arange(n_tiles) < n_used, tile_expert, last_expert)
    order = jnp.argsort(e_flat, stable=True).astype(jnp.int32)
    row_expert = jnp.repeat(tile_expert, tm)
    within = jnp.arange(p_rows, dtype=jnp.int32) - offsets[row_expert]
    src = (jnp.cumsum(counts) - counts)[row_expert] + within
    tok_sorted = jnp.where(within < counts[row_expert], order[jnp.clip(src, 0, TOP_K * n - 1)] % n, 0)
    x_sorted = jnp.take(xn, tok_sorted, axis=0)

    y_sorted = _moe_experts(tile_expert, n_used.reshape(1), x_sorted, w_gate_up, w_down)
    y = (gates[0][:, None] * jnp.take(y_sorted, pos[:n], axis=0).astype(F32)
         + gates[1][:, None] * jnp.take(y_sorted, pos[n:], axis=0).astype(F32))
    return h + y


def _sb_attn_kernel(q_ref, k_ref, v_ref, u_ref, o_ref, *, tb, heads):
    tri = u_ref[...]
    nq = q_ref.shape[1] // tb
    below_diag = (lax.broadcasted_iota(jnp.int32, (tb, tb), 1)
                  < lax.broadcasted_iota(jnp.int32, (tb, tb), 0))

    def block(q, g, jb, r_sum, acc, diagonal):
        start = _aligned(jb * tb, tb)
        kb = k_ref[0, pl.ds(start, tb), g * SB_HEAD:(g + 1) * SB_HEAD]
        vb = v_ref[0, pl.ds(start, tb), g * SB_HEAD:(g + 1) * SB_HEAD]
        z = _dot_nt(q, kb)
        neg_abs = pltpu.bitcast(pltpu.bitcast(z, jnp.uint32) | jnp.uint32(0x80000000), F32)
        soft = jnp.log(1.0 + jnp.exp2(neg_abs)) * LOG2E
        ls = jnp.minimum(z, 0.0) - soft
        lb = ls - z
        if diagonal:
            lb = jnp.where(below_diag, lb, 0.0)
        between = _dot(lb.astype(BF16), tri)
        a = jnp.exp2(ls + between + r_sum)
        if diagonal:
            a = jnp.where(below_diag, a, 0.0)
        acc = acc + _dot(a.astype(BF16), vb)
        r_sum = r_sum + jnp.sum(lb, axis=-1, keepdims=True)
        return r_sum, acc

    def q_block(i, first):
        qs = _aligned(i * tb, tb)
        qg = [q_ref[0, pl.ds(qs, tb), g * SB_HEAD:(g + 1) * SB_HEAD] for g in range(heads)]
        zero_r, zero_acc = jnp.zeros((tb, 1), F32), jnp.zeros((tb, SB_HEAD), F32)
        state = [block(qg[g], g, i, zero_r, zero_acc, True) for g in range(heads)]
        if not first:
            state = [block(qg[g], g, i - 1, state[g][0], state[g][1], False) for g in range(heads)]

            def cond(c):
                live = jnp.max(c[1][0][0])
                for g in range(1, heads):
                    live = jnp.maximum(live, jnp.max(c[1][g][0]))
                return jnp.logical_and(c[0] >= 0, live > SB_EXIT_LOG2)

            def body(c):
                return c[0] - 1, [block(qg[g], g, c[0], c[1][g][0], c[1][g][1], False) for g in range(heads)]

            _, state = lax.while_loop(cond, body, (i - 2, state))
        for g in range(heads):
            o_ref[0, pl.ds(qs, tb), g * SB_HEAD:(g + 1) * SB_HEAD] = state[g][1].astype(BF16)

    q_block(0, True)

    def rest(i, carry):
        q_block(i, False)
        return carry

    lax.fori_loop(1, nq, rest, 0)


def _sb_attn(q, kv, tri):
    b, s, _ = q.shape
    hg = SB_HEADS_PER_STEP
    ng = SB_HEADS // hg
    w = hg * SB_HEAD
    return pl.pallas_call(
        functools.partial(_sb_attn_kernel, tb=TB_SB, heads=hg),
        out_shape=jax.ShapeDtypeStruct((b, s, SB_HEADS * SB_HEAD), BF16),
        grid=(b, ng),
        in_specs=[pl.BlockSpec((1, s, w), lambda bi, h: (bi, 0, h)),
                  pl.BlockSpec((1, s, w), lambda bi, h: (bi, 0, h)),
                  pl.BlockSpec((1, s, w), lambda bi, h: (bi, 0, ng + h)),
                  pl.BlockSpec((TB_SB, TB_SB), lambda bi, h: (0, 0))],
        out_specs=pl.BlockSpec((1, s, w), lambda bi, h: (bi, 0, h)),
        compiler_params=_params("parallel", "parallel"),
        name="sb_attn",
    )(q, kv, kv, tri)


def _final_norm_kernel(h_ref, g_ref, o_ref):
    o_ref[...] = _rms(h_ref[...], g_ref[...])


def _final_norm(h, g):
    n = h.shape[0]
    tm = TM_PROJ
    return pl.pallas_call(
        _final_norm_kernel,
        out_shape=jax.ShapeDtypeStruct(h.shape, F32),
        grid=(n // tm,),
        in_specs=[pl.BlockSpec((tm, D_MODEL), lambda i: (i, 0)),
                  pl.BlockSpec((1, D_MODEL), lambda i: (0, 0))],
        out_specs=pl.BlockSpec((tm, D_MODEL), lambda i: (i, 0)),
        compiler_params=_params("parallel"),
        name="final_norm",
    )(h, g)


def _rope_tables(positions):
    inv_freq = 1.0 / (ROPE_BASE ** (jnp.arange(0, QK_ROPE, 2, dtype=F32) / QK_ROPE))
    ang = positions.astype(F32).reshape(-1, 1) * inv_freq
    cos, sin = jnp.cos(ang), jnp.sin(ang)
    z32 = jnp.zeros_like(cos)
    z64 = jnp.zeros((cos.shape[0], LANES - QK_ROPE), F32)
    cos_t = jnp.concatenate([cos, cos, z64], axis=-1)
    sin_a = jnp.concatenate([-sin, z32, z64], axis=-1)
    sin_b = jnp.concatenate([z32, sin, z64], axis=-1)
    return cos_t, sin_a, sin_b


def _mla_weights(w_down, w_uq, w_ukv):
    wd = jnp.pad(w_down, ((0, 0), (0, DOWN_PAD - w_down.shape[1]))).astype(BF16)
    wuq = w_uq.reshape(Q_LORA, MLA_HEADS, QK_NOPE + QK_ROPE)
    wuq = jnp.pad(wuq, ((0, 0), (0, 0), (0, MLA_QK_PAD - QK_NOPE - QK_ROPE)))
    wuq = wuq.reshape(Q_LORA, MLA_HEADS * MLA_QK_PAD).astype(BF16)
    wukv = w_ukv.reshape(KV_LORA, MLA_HEADS, QK_NOPE + V_HEAD)
    wuk = wukv[:, :, :QK_NOPE].reshape(KV_LORA, MLA_HEADS * QK_NOPE).astype(BF16)
    wuv = wukv[:, :, QK_NOPE:].reshape(KV_LORA, MLA_HEADS * V_HEAD).astype(BF16)
    return wd, wuq, wuk, wuv


def kernel(x, positions, attn_norm_0, mla_w_down_0, mla_q_norm_0, mla_w_uq_0, mla_kv_norm_0, mla_w_ukv_0, mla_w_o_0, ffn_norm_0, ffn_w_gate_up_0, ffn_w_down_0, attn_norm_1, mla_w_down_1, mla_q_norm_1, mla_w_uq_1, mla_kv_norm_1, mla_w_ukv_1, mla_w_o_1, ffn_norm_1, moe_router_1, moe_w_gate_up_1, moe_w_down_1, kv_shared_norm, kv_shared_w, attn_norm_2, sb_w_q_2, sb_w_o_2, ffn_norm_2, ffn_w_gate_up_2, ffn_w_down_2, attn_norm_3, sb_w_q_3, sb_w_o_3, ffn_norm_3, moe_router_3, moe_w_gate_up_3, moe_w_down_3, final_norm):
    b, s, d = x.shape
    n = b * s
    row = lambda g: g.reshape(1, -1)
    h = x.reshape(n, d)
    cos_t, sin_a, sin_b = _rope_tables(positions)

    mla_layers = [
        (attn_norm_0, mla_w_down_0, mla_q_norm_0, mla_w_uq_0, mla_kv_norm_0, mla_w_ukv_0, mla_w_o_0),
        (attn_norm_1, mla_w_down_1, mla_q_norm_1, mla_w_uq_1, mla_kv_norm_1, mla_w_ukv_1, mla_w_o_1),
    ]
    sb_layers = [(attn_norm_2, sb_w_q_2, sb_w_o_2), (attn_norm_3, sb_w_q_3, sb_w_o_3)]
    dense = {0: (ffn_norm_0, ffn_w_gate_up_0, ffn_w_down_0), 2: (ffn_norm_2, ffn_w_gate_up_2, ffn_w_down_2)}
    moe = {1: (ffn_norm_1, moe_router_1, moe_w_gate_up_1, moe_w_down_1),
           3: (ffn_norm_3, moe_router_3, moe_w_gate_up_3, moe_w_down_3)}
    tri = (lax.broadcasted_iota(jnp.int32, (TB_SB, TB_SB), 0)
           > lax.broadcasted_iota(jnp.int32, (TB_SB, TB_SB), 1)).astype(BF16)

    kv_shared = None
    for layer in range(4):
        if layer < 2:
            an, w_down, qn, w_uq, kvn, w_ukv, w_o = mla_layers[layer]
            wd, wuq, wuk, wuv = _mla_weights(w_down, w_uq, w_ukv)
            q, k, v = _mla_proj(h, row(an), wd, row(qn), row(kvn), wuq, wuk, wuv, cos_t, sin_a, sin_b)
            o = _mla_attn(q.reshape(b, s, -1), k.reshape(b, s, -1), v.reshape(b, s, -1))
            h = _proj_residual(h, o.reshape(n, -1), w_o.astype(BF16))
        else:
            if layer == 2:
                kv_shared = _norm_matmul(h, row(kv_shared_norm), kv_shared_w.astype(BF16)).reshape(b, s, -1)
            an, w_q, w_o = sb_layers[layer - 2]
            q = _norm_matmul(h, row(an), w_q.astype(BF16), scale=SB_HEAD ** -0.5 * LOG2E)
            o = _sb_attn(q.reshape(b, s, -1), kv_shared, tri)
            h = _proj_residual(h, o.reshape(n, -1), w_o.astype(BF16))
        if layer % 2 == 0:
            fn, w_gu, w_dn = dense[layer]
            h = _dense_ffn(h, row(fn), w_gu.astype(BF16), w_dn.astype(BF16))
        else:
            fn, router, w_gu, w_dn = moe[layer]
            h = _moe_layer(h, row(fn), router, w_gu.astype(BF16), w_dn.astype(BF16))
    return _final_norm(h, row(final_norm)).reshape(b, s, d)
```

```python
import functools

import jax
import jax.numpy as jnp
from jax import lax
from jax.experimental import pallas as pl
from jax.experimental.pallas import tpu as pltpu

F32 = jnp.float32
BF16 = jnp.bfloat16

D_MODEL = 1024
CHUNK = 64
MLA_HEADS = 8
QK_NOPE = 128
QK_ROPE = 64
V_HEAD = 128
Q_LORA = 384
KV_LORA = 256
ROPE_BASE = 10000.0
SB_HEADS = 8
SB_HEAD = 128
D_FF = 2816
N_EXPERTS = 8
TOP_K = 2
EPS = 1e-6

LANES = 128
MLA_QK_PAD = 256
DOWN_PAD = 768
VMEM_LIMIT = 56 * 1024 * 1024

TM_PROJ = 512
TM_FFN = 512
TM_MOE = 256
CAST_BLOCK_BYTES = 6 * 1024 * 1024
TQ_MLA = 512
TK_MLA = 512
MLA_ROW_CHUNK = 32
MLA_HEADS_PER_STEP = 4
TB_SB = 256
SB_HEADS_PER_STEP = 4
TM_ROUTER = 512
LOG2E = 1.4426950408889634
SB_EXIT_LOG2 = -150.0


def _rms(x, g):
    ms = jnp.mean(x * x, axis=-1, keepdims=True)
    return x * lax.rsqrt(ms + EPS) * g


def _dot(a, b):
    return jnp.dot(a, b, preferred_element_type=F32)


def _dot_nt(a, b):
    return lax.dot_general(a, b, (((1,), (1,)), ((), ())), preferred_element_type=F32)


def _silu(g):
    return g / (1.0 + jnp.exp(-g))


def _aligned(x, m):
    return x if isinstance(x, int) else pl.multiple_of(x, m)


def _params(*sem):
    return pltpu.CompilerParams(dimension_semantics=sem, vmem_limit_bytes=VMEM_LIMIT)


def _mla_proj_kernel(h_ref, g_ref, wd_ref, qn_ref, kvn_ref, wuq_ref, wuk_ref, wuv_ref,
                     c_ref, sa_ref, sb_ref, q_ref, k_ref, v_ref, *, scale):
    xn = _rms(h_ref[...], g_ref[...]).astype(BF16)
    c = _dot(xn, wd_ref[...])
    cq = _rms(c[:, :Q_LORA], qn_ref[...]).astype(BF16)
    ckv = _rms(c[:, Q_LORA:Q_LORA + KV_LORA], kvn_ref[...]).astype(BF16)
    cos_t, sin_a, sin_b = c_ref[...], sa_ref[...], sb_ref[...]

    def rope(r):
        return r * cos_t + pltpu.roll(r, 96, 1) * sin_a + pltpu.roll(r, 32, 1) * sin_b

    kr = rope(c[:, Q_LORA + KV_LORA:]).astype(BF16)
    q = _dot(cq, wuq_ref[...])
    kn = _dot(ckv, wuk_ref[...])
    v_ref[...] = _dot(ckv, wuv_ref[...]).astype(BF16)
    for h in range(MLA_HEADS):
        lo = h * MLA_QK_PAD
        q_ref[:, lo:lo + LANES] = (q[:, lo:lo + LANES] * scale).astype(BF16)
        q_ref[:, lo + LANES:lo + 2 * LANES] = (rope(q[:, lo + LANES:lo + 2 * LANES]) * scale).astype(BF16)
        k_ref[:, lo:lo + LANES] = kn[:, h * LANES:(h + 1) * LANES].astype(BF16)
        k_ref[:, lo + LANES:lo + 2 * LANES] = kr


def _mla_proj(h, g, wd, qn, kvn, wuq, wuk, wuv, cos_t, sin_a, sin_b):
    n = h.shape[0]
    tm = TM_PROJ
    row = lambda i: (i, 0)
    fixed = lambda i: (0, 0)
    scale = float((QK_NOPE + QK_ROPE) ** -0.5 * LOG2E)
    return pl.pallas_call(
        functools.partial(_mla_proj_kernel, scale=scale),
        out_shape=(jax.ShapeDtypeStruct((n, MLA_HEADS * MLA_QK_PAD), BF16),
                   jax.ShapeDtypeStruct((n, MLA_HEADS * MLA_QK_PAD), BF16),
                   jax.ShapeDtypeStruct((n, MLA_HEADS * V_HEAD), BF16)),
        grid=(n // tm,),
        in_specs=[pl.BlockSpec((tm, D_MODEL), row),
                  pl.BlockSpec((1, D_MODEL), fixed),
                  pl.BlockSpec((D_MODEL, DOWN_PAD), fixed),
                  pl.BlockSpec((1, Q_LORA), fixed),
                  pl.BlockSpec((1, KV_LORA), fixed),
                  pl.BlockSpec((Q_LORA, MLA_HEADS * MLA_QK_PAD), fixed),
                  pl.BlockSpec((KV_LORA, MLA_HEADS * QK_NOPE), fixed),
                  pl.BlockSpec((KV_LORA, MLA_HEADS * V_HEAD), fixed),
                  pl.BlockSpec((tm, LANES), row),
                  pl.BlockSpec((tm, LANES), row),
                  pl.BlockSpec((tm, LANES), row)],
        out_specs=(pl.BlockSpec((tm, MLA_HEADS * MLA_QK_PAD), row),
                   pl.BlockSpec((tm, MLA_HEADS * MLA_QK_PAD), row),
                   pl.BlockSpec((tm, MLA_HEADS * V_HEAD), row)),
        compiler_params=_params("parallel"),
        name="mla_proj",
    )(h, g, wd, qn, kvn, wuq, wuk, wuv, cos_t, sin_a, sin_b)


def _mla_attn_kernel(q_ref, k_ref, v_ref, o_ref, s_ref, p_ref, m_ref, l_ref, alpha_ref, acc_ref, *, tq, tk, heads):
    i = pl.program_id(2)
    qg = [q_ref[0, :, g * MLA_QK_PAD:(g + 1) * MLA_QK_PAD] for g in range(heads)]
    rc = MLA_ROW_CHUNK

    m_ref[...] = jnp.full(m_ref.shape, -jnp.inf, F32)
    l_ref[...] = jnp.zeros(l_ref.shape, F32)
    acc_ref[...] = jnp.zeros(acc_ref.shape, F32)

    def scores(g, j):
        kb = k_ref[0, pl.ds(_aligned(j * tk, tk), tk), g * MLA_QK_PAD:(g + 1) * MLA_QK_PAD]
        s_ref[g] = _dot_nt(qg[g], kb)

    def update(g, j, diagonal):
        vb = v_ref[0, pl.ds(_aligned(j * tk, tk), tk), g * V_HEAD:(g + 1) * V_HEAD]
        for c in range(tq // rc):
            rows = slice(c * rc, (c + 1) * rc)
            s = s_ref[g, rows, :]
            if diagonal:
                qc = (lax.broadcasted_iota(jnp.int32, (rc, tk), 0) + c * rc) // CHUNK
                kc = lax.broadcasted_iota(jnp.int32, (rc, tk), 1) // CHUNK
                s = jnp.where(kc <= qc, s, -jnp.inf)
            m_old = m_ref[g, rows, :]
            m_new = jnp.maximum(m_old, jnp.max(s, axis=-1, keepdims=True))
            alpha = jnp.exp2(m_old - m_new)
            m_ref[g, rows, :] = m_new
            alpha_ref[g, rows, :] = alpha
            psum = jnp.zeros((rc, LANES), F32)
            for t in range(tk // LANES):
                p = jnp.exp2(s[:, t * LANES:(t + 1) * LANES] - m_new)
                psum = psum + p
                p_ref[g, rows, t * LANES:(t + 1) * LANES] = p.astype(BF16)
            l_ref[g, rows, :] = alpha * l_ref[g, rows, :] + jnp.sum(psum, axis=-1, keepdims=True)
        acc_ref[g] = alpha_ref[g] * acc_ref[g] + _dot(p_ref[g], vb)

    for g in range(heads):
        scores(g, 0)

    def body(j, carry):
        for g in range(heads):
            update(g, j, False)
            scores(g, j + 1)
        return carry

    lax.fori_loop(0, i, body, 0)
    for g in range(heads):
        update(g, i, True)
    for g in range(heads):
        o_ref[0, :, g * V_HEAD:(g + 1) * V_HEAD] = (acc_ref[g] / l_ref[g]).astype(BF16)


def _mla_attn(q, k, v):
    b, s, _ = q.shape
    tq, tk, hg = TQ_MLA, TK_MLA, MLA_HEADS_PER_STEP
    return pl.pallas_call(
        functools.partial(_mla_attn_kernel, tq=tq, tk=tk, heads=hg),
        out_shape=jax.ShapeDtypeStruct((b, s, MLA_HEADS * V_HEAD), BF16),
        grid=(b, MLA_HEADS // hg, s // tq),
        in_specs=[pl.BlockSpec((1, tq, hg * MLA_QK_PAD), lambda bi, h, i: (bi, i, h)),
                  pl.BlockSpec((1, s, hg * MLA_QK_PAD), lambda bi, h, i: (bi, 0, h)),
                  pl.BlockSpec((1, s, hg * V_HEAD), lambda bi, h, i: (bi, 0, h))],
        out_specs=pl.BlockSpec((1, tq, hg * V_HEAD), lambda bi, h, i: (bi, i, h)),
        scratch_shapes=[pltpu.VMEM((hg, tq, tk), F32), pltpu.VMEM((hg, tq, tk), BF16),
                        pltpu.VMEM((hg, tq, LANES), F32), pltpu.VMEM((hg, tq, LANES), F32),
                        pltpu.VMEM((hg, tq, LANES), F32), pltpu.VMEM((hg, tq, V_HEAD), F32)],
        compiler_params=_params("parallel", "parallel", "arbitrary"),
        name="mla_attn",
    )(q, k, v)


def _cast_kernel(x_ref, o_ref):
    o_ref[...] = x_ref[...].astype(BF16)


def _to_bf16(w):
    cols = w.shape[-1]
    rows = w.size // cols
    tr = max(r for r in range(16, rows + 1, 16) if rows % r == 0 and r * cols * 4 <= CAST_BLOCK_BYTES)
    out = pl.pallas_call(
        _cast_kernel,
        out_shape=jax.ShapeDtypeStruct((rows, cols), BF16),
        grid=(rows // tr,),
        in_specs=[pl.BlockSpec((tr, cols), lambda i: (i, 0))],
        out_specs=pl.BlockSpec((tr, cols), lambda i: (i, 0)),
        compiler_params=_params("parallel"),
        name="cast_bf16",
    )(w.reshape(rows, cols))
    return out.reshape(w.shape)


def _norm_matmul_kernel(h_ref, g_ref, w_ref, o_ref, *, scale):
    xn = _rms(h_ref[...], g_ref[...]).astype(BF16)
    o_ref[...] = (_dot(xn, w_ref[...]) * scale).astype(BF16)


def _norm_matmul(h, g, w, scale=1.0):
    n = h.shape[0]
    tm = TM_PROJ
    f = w.shape[1]
    row = lambda i: (i, 0)
    return pl.pallas_call(
        functools.partial(_norm_matmul_kernel, scale=float(scale)),
        out_shape=jax.ShapeDtypeStruct((n, f), BF16),
        grid=(n // tm,),
        in_specs=[pl.BlockSpec((tm, D_MODEL), row),
                  pl.BlockSpec((1, D_MODEL), lambda i: (0, 0)),
                  pl.BlockSpec(w.shape, lambda i: (0, 0))],
        out_specs=pl.BlockSpec((tm, f), row),
        compiler_params=_params("parallel"),
        name="norm_matmul",
    )(h, g, w)


def _swiglu(x, wgu_ref, wd_ref):
    gate = _dot(x, wgu_ref[:, :D_FF])
    up = _dot(x, wgu_ref[:, D_FF:])
    return _dot((_silu(gate) * up).astype(BF16), wd_ref[...])


def _ffn_kernel(h_ref, a_ref, wo_ref, g_ref, wgu_ref, wd_ref, o_ref):
    h = h_ref[...] + _dot(a_ref[...], wo_ref[...])
    x = _rms(h, g_ref[...]).astype(BF16)
    o_ref[...] = h + _swiglu(x, wgu_ref, wd_ref)


def _resident(shape):
    return pl.BlockSpec(shape, lambda *_: (0,) * len(shape), pipeline_mode=pl.Buffered(1))


def _dense_ffn(h, a, w_o, g, w_gate_up, w_down):
    n = h.shape[0]
    tm = TM_FFN
    row = lambda i: (i, 0)
    return pl.pallas_call(
        _ffn_kernel,
        out_shape=jax.ShapeDtypeStruct(h.shape, F32),
        grid=(n // tm,),
        in_specs=[pl.BlockSpec((tm, D_MODEL), row),
                  pl.BlockSpec((tm, D_MODEL), row),
                  _resident(w_o.shape),
                  _resident(g.shape),
                  _resident(w_gate_up.shape),
                  _resident(w_down.shape)],
        out_specs=pl.BlockSpec((tm, D_MODEL), row),
        input_output_aliases={0: 0},
        compiler_params=_params("parallel"),
        name="dense_ffn",
    )(h, a, w_o, g, w_gate_up, w_down)


def _router_kernel(h_ref, a_ref, wo_ref, g_ref, rhi_ref, rlo_ref, h_out_ref, xn_ref, idx_ref, gate_ref):
    h = h_ref[...] + _dot(a_ref[...], wo_ref[...])
    h_out_ref[...] = h
    xn = _rms(h, g_ref[...])
    x_hi = xn.astype(BF16)
    x_lo = (xn - x_hi.astype(F32)).astype(BF16)
    xn_ref[...] = x_hi
    logits = _dot_nt(rhi_ref[...], x_hi) + _dot_nt(rhi_ref[...], x_lo) + _dot_nt(rlo_ref[...], x_hi)
    m = jnp.max(logits, axis=0, keepdims=True)
    e = jnp.exp(logits - m)
    p = e / jnp.sum(e, axis=0, keepdims=True)
    eid = lax.broadcasted_iota(jnp.int32, p.shape, 0)
    p1 = jnp.max(p, axis=0, keepdims=True)
    i1 = jnp.min(jnp.where(p == p1, eid, N_EXPERTS), axis=0, keepdims=True)
    pm = jnp.where(eid == i1, -1.0, p)
    p2 = jnp.max(pm, axis=0, keepdims=True)
    i2 = jnp.min(jnp.where(pm == p2, eid, N_EXPERTS), axis=0, keepdims=True)
    den = p1 + p2
    idx_ref[...] = jnp.concatenate([i1, i2], axis=0)
    gate_ref[...] = jnp.concatenate([p1 / den, p2 / den], axis=0)


def _router(h, a, w_o, g, r_hi, r_lo):
    n = h.shape[0]
    tm = TM_ROUTER
    row = lambda i: (i, 0)
    return pl.pallas_call(
        _router_kernel,
        out_shape=(jax.ShapeDtypeStruct((n, D_MODEL), F32),
                   jax.ShapeDtypeStruct((n, D_MODEL), BF16),
                   jax.ShapeDtypeStruct((TOP_K, n), jnp.int32),
                   jax.ShapeDtypeStruct((TOP_K, n), F32)),
        grid=(n // tm,),
        in_specs=[pl.BlockSpec((tm, D_MODEL), row),
                  pl.BlockSpec((tm, D_MODEL), row),
                  _resident(w_o.shape),
                  _resident(g.shape),
                  _resident(r_hi.shape),
                  _resident(r_lo.shape)],
        out_specs=(pl.BlockSpec((tm, D_MODEL), row),
                   pl.BlockSpec((tm, D_MODEL), row),
                   pl.BlockSpec((TOP_K, tm), lambda i: (0, i)),
                   pl.BlockSpec((TOP_K, tm), lambda i: (0, i))),
        input_output_aliases={0: 0},
        compiler_params=_params("parallel"),
        name="moe_router",
    )(h, a, w_o, g, r_hi, r_lo)


def _moe_kernel(te_ref, nu_ref, x_ref, wgu_ref, wd_ref, o_ref):
    used = pl.program_id(0) < nu_ref[0]

    @pl.when(used)
    def _():
        o_ref[...] = _swiglu(x_ref[...], wgu_ref.at[0], wd_ref.at[0]).astype(BF16)

    @pl.when(jnp.logical_not(used))
    def _():
        o_ref[...] = jnp.zeros(o_ref.shape, BF16)


def _moe_experts(tile_expert, n_used, x_sorted, w_gate_up, w_down):
    p = x_sorted.shape[0]
    tm = TM_MOE
    grid_spec = pltpu.PrefetchScalarGridSpec(
        num_scalar_prefetch=2,
        grid=(p // tm,),
        in_specs=[pl.BlockSpec((tm, D_MODEL), lambda t, te, nu: (t, 0)),
                  pl.BlockSpec((1, D_MODEL, 2 * D_FF), lambda t, te, nu: (te[t], 0, 0)),
                  pl.BlockSpec((1, D_FF, D_MODEL), lambda t, te, nu: (te[t], 0, 0))],
        out_specs=pl.BlockSpec((tm, D_MODEL), lambda t, te, nu: (t, 0)),
    )
    return pl.pallas_call(
        _moe_kernel,
        out_shape=jax.ShapeDtypeStruct((p, D_MODEL), BF16),
        grid_spec=grid_spec,
        compiler_params=_params("arbitrary"),
        name="moe_experts",
    )(tile_expert, n_used, x_sorted, w_gate_up, w_down)


def _moe_dispatch(h, a, w_o, g, router, w_gate_up, w_down):
    n = h.shape[0]
    tm = TM_MOE
    r_t = router.T
    r_hi = r_t.astype(BF16)
    r_lo = (r_t - r_hi.astype(F32)).astype(BF16)
    h, xn, idx, gates = _router(h, a, w_o, g, r_hi, r_lo)

    e_flat = idx.reshape(-1)
    onehot = (e_flat[:, None] == jnp.arange(N_EXPERTS, dtype=jnp.int32)[None, :]).astype(jnp.int32)
    csum = jnp.cumsum(onehot, axis=0)
    rank = jnp.sum(csum * onehot, axis=1) - 1
    counts = csum[-1]
    padded = ((counts + tm - 1) // tm) * tm
    ends = jnp.cumsum(padded)
    offsets = ends - padded
    pos = offsets[e_flat] + rank
    p_rows = TOP_K * n + N_EXPERTS * tm
    n_tiles = p_rows // tm
    n_used = (ends[-1] // tm).astype(jnp.int32)
    tile_start = jnp.arange(n_tiles, dtype=jnp.int32) * tm
    tile_expert = jnp.sum((tile_start[:, None] >= ends[None, :]).astype(jnp.int32), axis=1)
    tile_expert = jnp.minimum(tile_expert, N_EXPERTS - 1)
    last_expert = tile_expert[jnp.maximum(n_used - 1, 0)]
    tile_expert = jnp.where(jnp.arange(n_tiles) < n_used, tile_expert, last_expert)
    order = jnp.argsort(e_flat, stable=True).astype(jnp.int32)
    row_expert = jnp.repeat(tile_expert, tm)
    within = jnp.arange(p_rows, dtype=jnp.int32) - offsets[row_expert]
    src = (jnp.cumsum(counts) - counts)[row_expert] + within
    tok_sorted = jnp.where(within < counts[row_expert], order[jnp.clip(src, 0, TOP_K * n - 1)] % n, 0)
    x_sorted = jnp.take(xn, tok_sorted, axis=0)

    y_sorted = _moe_experts(tile_expert, n_used.reshape(1), x_sorted, w_gate_up, w_down)
    return h, jnp.take(y_sorted, pos[:n], axis=0), jnp.take(y_sorted, pos[n:], axis=0), gates.T


def _mix(h_ref, y0_ref, y1_ref, gt_ref):
    gt = gt_ref[...]
    return h_ref[...] + gt[:, 0:1] * y0_ref[...].astype(F32) + gt[:, 1:2] * y1_ref[...].astype(F32)


def _combine_proj_kernel(h_ref, y0_ref, y1_ref, gt_ref, gk_ref, gq_ref, wkv_ref, wq_ref,
                         h_out_ref, kv_ref, q_ref, *, q_scale):
    h = _mix(h_ref, y0_ref, y1_ref, gt_ref)
    h_out_ref[...] = h
    xhat = h * lax.rsqrt(jnp.mean(h * h, axis=-1, keepdims=True) + EPS)
    kv_ref[...] = _dot((xhat * gk_ref[...]).astype(BF16), wkv_ref[...]).astype(BF16)
    q_ref[...] = (_dot((xhat * gq_ref[...]).astype(BF16), wq_ref[...]) * q_scale).astype(BF16)


def _combine_proj(h, y0, y1, gt, g_kv, g_q, w_kv, w_q, q_scale):
    n = h.shape[0]
    tm = TM_PROJ
    row = lambda i: (i, 0)
    return pl.pallas_call(
        functools.partial(_combine_proj_kernel, q_scale=float(q_scale)),
        out_shape=(jax.ShapeDtypeStruct(h.shape, F32),
                   jax.ShapeDtypeStruct((n, w_kv.shape[1]), BF16),
                   jax.ShapeDtypeStruct((n, w_q.shape[1]), BF16)),
        grid=(n // tm,),
        in_specs=[pl.BlockSpec((tm, D_MODEL), row),
                  pl.BlockSpec((tm, D_MODEL), row),
                  pl.BlockSpec((tm, D_MODEL), row),
                  pl.BlockSpec((tm, TOP_K), row),
                  _resident(g_kv.shape),
                  _resident(g_q.shape),
                  _resident(w_kv.shape),
                  _resident(w_q.shape)],
        out_specs=(pl.BlockSpec((tm, D_MODEL), row),
                   pl.BlockSpec((tm, w_kv.shape[1]), row),
                   pl.BlockSpec((tm, w_q.shape[1]), row)),
        input_output_aliases={0: 0},
        compiler_params=_params("parallel"),
        name="combine_proj",
    )(h, y0, y1, gt, g_kv, g_q, w_kv, w_q)


def _combine_norm_kernel(h_ref, y0_ref, y1_ref, gt_ref, g_ref, o_ref):
    o_ref[...] = _rms(_mix(h_ref, y0_ref, y1_ref, gt_ref), g_ref[...])


def _combine_norm(h, y0, y1, gt, g):
    n = h.shape[0]
    tm = TM_PROJ
    row = lambda i: (i, 0)
    return pl.pallas_call(
        _combine_norm_kernel,
        out_shape=jax.ShapeDtypeStruct(h.shape, F32),
        grid=(n // tm,),
        in_specs=[pl.BlockSpec((tm, D_MODEL), row),
                  pl.BlockSpec((tm, D_MODEL), row),
                  pl.BlockSpec((tm, D_MODEL), row),
                  pl.BlockSpec((tm, TOP_K), row),
                  _resident(g.shape)],
        out_specs=pl.BlockSpec((tm, D_MODEL), row),
        input_output_aliases={0: 0},
        compiler_params=_params("parallel"),
        name="combine_norm",
    )(h, y0, y1, gt, g)


def _sb_attn_kernel(q_ref, k_ref, v_ref, u_ref, o_ref, *, tb, heads):
    tri = u_ref[...]
    nq = q_ref.shape[1] // tb
    below_diag = (lax.broadcasted_iota(jnp.int32, (tb, tb), 1)
                  < lax.broadcasted_iota(jnp.int32, (tb, tb), 0))

    def block(q, g, jb, r_sum, acc, diagonal):
        start = _aligned(jb * tb, tb)
        kb = k_ref[0, pl.ds(start, tb), g * SB_HEAD:(g + 1) * SB_HEAD]
        vb = v_ref[0, pl.ds(start, tb), g * SB_HEAD:(g + 1) * SB_HEAD]
        z = _dot_nt(q, kb)
        neg_abs = pltpu.bitcast(pltpu.bitcast(z, jnp.uint32) | jnp.uint32(0x80000000), F32)
        soft = jnp.log(1.0 + jnp.exp2(neg_abs)) * LOG2E
        ls = jnp.minimum(z, 0.0) - soft
        lb = ls - z
        if diagonal:
            lb = jnp.where(below_diag, lb, 0.0)
        between = _dot(lb.astype(BF16), tri)
        a = jnp.exp2(ls + between + r_sum)
        if diagonal:
            a = jnp.where(below_diag, a, 0.0)
        acc = acc + _dot(a.astype(BF16), vb)
        r_sum = r_sum + jnp.sum(lb, axis=-1, keepdims=True)
        return r_sum, acc

    def q_block(i, first):
        qs = _aligned(i * tb, tb)
        qg = [q_ref[0, pl.ds(qs, tb), g * SB_HEAD:(g + 1) * SB_HEAD] for g in range(heads)]
        zero_r, zero_acc = jnp.zeros((tb, 1), F32), jnp.zeros((tb, SB_HEAD), F32)
        state = [block(qg[g], g, i, zero_r, zero_acc, True) for g in range(heads)]
        if not first:
            state = [block(qg[g], g, i - 1, state[g][0], state[g][1], False) for g in range(heads)]

            def cond(c):
                live = jnp.max(c[1][0][0])
                for g in range(1, heads):
                    live = jnp.maximum(live, jnp.max(c[1][g][0]))
                return jnp.logical_and(c[0] >= 0, live > SB_EXIT_LOG2)

            def body(c):
                return c[0] - 1, [block(qg[g], g, c[0], c[1][g][0], c[1][g][1], False) for g in range(heads)]

            _, state = lax.while_loop(cond, body, (i - 2, state))
        for g in range(heads):
            o_ref[0, pl.ds(qs, tb), g * SB_HEAD:(g + 1) * SB_HEAD] = state[g][1].astype(BF16)

    q_block(0, True)

    def rest(i, carry):
        q_block(i, False)
        return carry

    lax.fori_loop(1, nq, rest, 0)


def _sb_attn(q, kv, tri):
    b, s, _ = q.shape
    hg = SB_HEADS_PER_STEP
    ng = SB_HEADS // hg
    w = hg * SB_HEAD
    return pl.pallas_call(
        functools.partial(_sb_attn_kernel, tb=TB_SB, heads=hg),
        out_shape=jax.ShapeDtypeStruct((b, s, SB_HEADS * SB_HEAD), BF16),
        grid=(b, ng),
        in_specs=[pl.BlockSpec((1, s, w), lambda bi, h: (bi, 0, h)),
                  pl.BlockSpec((1, s, w), lambda bi, h: (bi, 0, h)),
                  pl.BlockSpec((1, s, w), lambda bi, h: (bi, 0, ng + h)),
                  pl.BlockSpec((TB_SB, TB_SB), lambda bi, h: (0, 0))],
        out_specs=pl.BlockSpec((1, s, w), lambda bi, h: (bi, 0, h)),
        compiler_params=_params("parallel", "parallel"),
        name="sb_attn",
    )(q, kv, kv, tri)


def _rope_tables(positions):
    inv_freq = 1.0 / (ROPE_BASE ** (jnp.arange(0, QK_ROPE, 2, dtype=F32) / QK_ROPE))
    ang = positions.astype(F32).reshape(-1, 1) * inv_freq
    cos, sin = jnp.cos(ang), jnp.sin(ang)
    z32 = jnp.zeros_like(cos)
    z64 = jnp.zeros((cos.shape[0], LANES - QK_ROPE), F32)
    cos_t = jnp.concatenate([cos, cos, z64], axis=-1)
    sin_a = jnp.concatenate([-sin, z32, z64], axis=-1)
    sin_b = jnp.concatenate([z32, sin, z64], axis=-1)
    return cos_t, sin_a, sin_b


def _mla_weights(w_down, w_uq, w_ukv):
    wd = jnp.pad(w_down, ((0, 0), (0, DOWN_PAD - w_down.shape[1]))).astype(BF16)
    wuq = w_uq.reshape(Q_LORA, MLA_HEADS, QK_NOPE + QK_ROPE)
    wuq = jnp.pad(wuq, ((0, 0), (0, 0), (0, MLA_QK_PAD - QK_NOPE - QK_ROPE)))
    wuq = wuq.reshape(Q_LORA, MLA_HEADS * MLA_QK_PAD).astype(BF16)
    wukv = w_ukv.reshape(KV_LORA, MLA_HEADS, QK_NOPE + V_HEAD)
    wuk = wukv[:, :, :QK_NOPE].reshape(KV_LORA, MLA_HEADS * QK_NOPE).astype(BF16)
    wuv = wukv[:, :, QK_NOPE:].reshape(KV_LORA, MLA_HEADS * V_HEAD).astype(BF16)
    return wd, wuq, wuk, wuv


def kernel(x, positions, attn_norm_0, mla_w_down_0, mla_q_norm_0, mla_w_uq_0, mla_kv_norm_0, mla_w_ukv_0, mla_w_o_0, ffn_norm_0, ffn_w_gate_up_0, ffn_w_down_0, attn_norm_1, mla_w_down_1, mla_q_norm_1, mla_w_uq_1, mla_kv_norm_1, mla_w_ukv_1, mla_w_o_1, ffn_norm_1, moe_router_1, moe_w_gate_up_1, moe_w_down_1, kv_shared_norm, kv_shared_w, attn_norm_2, sb_w_q_2, sb_w_o_2, ffn_norm_2, ffn_w_gate_up_2, ffn_w_down_2, attn_norm_3, sb_w_q_3, sb_w_o_3, ffn_norm_3, moe_router_3, moe_w_gate_up_3, moe_w_down_3, final_norm):
    b, s, d = x.shape
    n = b * s
    row = lambda g: g.reshape(1, -1)
    h = x.reshape(n, d)
    cos_t, sin_a, sin_b = _rope_tables(positions)

    mla_layers = [
        (attn_norm_0, mla_w_down_0, mla_q_norm_0, mla_w_uq_0, mla_kv_norm_0, mla_w_ukv_0, mla_w_o_0),
        (attn_norm_1, mla_w_down_1, mla_q_norm_1, mla_w_uq_1, mla_kv_norm_1, mla_w_ukv_1, mla_w_o_1),
    ]
    tri = (lax.broadcasted_iota(jnp.int32, (TB_SB, TB_SB), 0)
           > lax.broadcasted_iota(jnp.int32, (TB_SB, TB_SB), 1)).astype(BF16)
    sb_scale = SB_HEAD ** -0.5 * LOG2E

    def mla_attention(h, layer):
        an, w_down, qn, w_uq, kvn, w_ukv, _ = mla_layers[layer]
        wd, wuq, wuk, wuv = _mla_weights(w_down, w_uq, w_ukv)
        q, k, v = _mla_proj(h, row(an), wd, row(qn), row(kvn), wuq, wuk, wuv, cos_t, sin_a, sin_b)
        return _mla_attn(q.reshape(b, s, -1), k.reshape(b, s, -1), v.reshape(b, s, -1)).reshape(n, -1)

    o = mla_attention(h, 0)
    h = _dense_ffn(h, o, mla_w_o_0.astype(BF16), row(ffn_norm_0), _to_bf16(ffn_w_gate_up_0), _to_bf16(ffn_w_down_0))
    o = mla_attention(h, 1)
    h, y0, y1, gt = _moe_dispatch(h, o, mla_w_o_1.astype(BF16), row(ffn_norm_1), moe_router_1,
                                  _to_bf16(moe_w_gate_up_1), _to_bf16(moe_w_down_1))
    h, kv_shared, q = _combine_proj(h, y0, y1, gt, row(kv_shared_norm), row(attn_norm_2),
                                    kv_shared_w.astype(BF16), sb_w_q_2.astype(BF16), sb_scale)
    kv_shared = kv_shared.reshape(b, s, -1)
    o = _sb_attn(q.reshape(b, s, -1), kv_shared, tri).reshape(n, -1)
    h = _dense_ffn(h, o, sb_w_o_2.astype(BF16), row(ffn_norm_2), _to_bf16(ffn_w_gate_up_2), _to_bf16(ffn_w_down_2))
    q = _norm_matmul(h, row(attn_norm_3), sb_w_q_3.astype(BF16), scale=sb_scale)
    o = _sb_attn(q.reshape(b, s, -1), kv_shared, tri).reshape(n, -1)
    h, y0, y1, gt = _moe_dispatch(h, o, sb_w_o_3.astype(BF16), row(ffn_norm_3), moe_router_3,
                                  _to_bf16(moe_w_gate_up_3), _to_bf16(moe_w_down_3))
    return _combine_norm(h, y0, y1, gt, row(final_norm)).reshape(b, s, d)
```

```python
import functools

import jax
import jax.numpy as jnp
from jax import lax
from jax.experimental import pallas as pl
from jax.experimental.pallas import tpu as pltpu

F32 = jnp.float32
BF16 = jnp.bfloat16

D_MODEL = 1024
CHUNK = 64
MLA_HEADS = 8
QK_NOPE = 128
QK_ROPE = 64
V_HEAD = 128
Q_LORA = 384
KV_LORA = 256
ROPE_BASE = 10000.0
SB_HEADS = 8
SB_HEAD = 128
D_FF = 2816
N_EXPERTS = 8
TOP_K = 2
EPS = 1e-6

LANES = 128
MLA_QK_PAD = 256
DOWN_PAD = 768
VMEM_LIMIT = 56 * 1024 * 1024

TM_PROJ = 512
TM_FFN = 512
TM_MOE = 256
CAST_BLOCK_BYTES = 6 * 1024 * 1024
TQ_MLA = 512
TK_MLA = 512
MLA_ROW_CHUNK = 32
MLA_HEADS_PER_STEP = 4
TB_SB = 256
SB_HEADS_PER_STEP = 4
TM_ROUTER = 512
LOG2E = 1.4426950408889634
SB_EXIT_LOG2 = -150.0


def _rms(x, g):
    ms = jnp.mean(x * x, axis=-1, keepdims=True)
    return x * lax.rsqrt(ms + EPS) * g


def _dot(a, b):
    return jnp.dot(a, b, preferred_element_type=F32)


def _dot_nt(a, b):
    return lax.dot_general(a, b, (((1,), (1,)), ((), ())), preferred_element_type=F32)


def _silu(g):
    return g / (1.0 + jnp.exp(-g))


def _aligned(x, m):
    return x if isinstance(x, int) else pl.multiple_of(x, m)


def _params(*sem):
    return pltpu.CompilerParams(dimension_semantics=sem, vmem_limit_bytes=VMEM_LIMIT)


def _mla_proj_kernel(h_ref, g_ref, wd_ref, qn_ref, kvn_ref, wuq_ref, wuk_ref, wuv_ref,
                     c_ref, sa_ref, sb_ref, q_ref, k_ref, v_ref, *, scale):
    xn = _rms(h_ref[...], g_ref[...]).astype(BF16)
    c = _dot(xn, wd_ref[...])
    cq = _rms(c[:, :Q_LORA], qn_ref[...]).astype(BF16)
    ckv = _rms(c[:, Q_LORA:Q_LORA + KV_LORA], kvn_ref[...]).astype(BF16)
    cos_t, sin_a, sin_b = c_ref[...], sa_ref[...], sb_ref[...]

    def rope(r):
        return r * cos_t + pltpu.roll(r, 96, 1) * sin_a + pltpu.roll(r, 32, 1) * sin_b

    kr = rope(c[:, Q_LORA + KV_LORA:]).astype(BF16)
    q = _dot(cq, wuq_ref[...])
    kn = _dot(ckv, wuk_ref[...])
    v_ref[...] = _dot(ckv, wuv_ref[...]).astype(BF16)
    for h in range(MLA_HEADS):
        lo = h * MLA_QK_PAD
        q_ref[:, lo:lo + LANES] = (q[:, lo:lo + LANES] * scale).astype(BF16)
        q_ref[:, lo + LANES:lo + 2 * LANES] = (rope(q[:, lo + LANES:lo + 2 * LANES]) * scale).astype(BF16)
        k_ref[:, lo:lo + LANES] = kn[:, h * LANES:(h + 1) * LANES].astype(BF16)
        k_ref[:, lo + LANES:lo + 2 * LANES] = kr


def _mla_proj(h, g, wd, qn, kvn, wuq, wuk, wuv, cos_t, sin_a, sin_b):
    n = h.shape[0]
    tm = TM_PROJ
    row = lambda i: (i, 0)
    fixed = lambda i: (0, 0)
    scale = float((QK_NOPE + QK_ROPE) ** -0.5 * LOG2E)
    return pl.pallas_call(
        functools.partial(_mla_proj_kernel, scale=scale),
        out_shape=(jax.ShapeDtypeStruct((n, MLA_HEADS * MLA_QK_PAD), BF16),
                   jax.ShapeDtypeStruct((n, MLA_HEADS * MLA_QK_PAD), BF16),
                   jax.ShapeDtypeStruct((n, MLA_HEADS * V_HEAD), BF16)),
        grid=(n // tm,),
        in_specs=[pl.BlockSpec((tm, D_MODEL), row),
                  pl.BlockSpec((1, D_MODEL), fixed),
                  pl.BlockSpec((D_MODEL, DOWN_PAD), fixed),
                  pl.BlockSpec((1, Q_LORA), fixed),
                  pl.BlockSpec((1, KV_LORA), fixed),
                  pl.BlockSpec((Q_LORA, MLA_HEADS * MLA_QK_PAD), fixed),
                  pl.BlockSpec((KV_LORA, MLA_HEADS * QK_NOPE), fixed),
                  pl.BlockSpec((KV_LORA, MLA_HEADS * V_HEAD), fixed),
                  pl.BlockSpec((tm, LANES), row),
                  pl.BlockSpec((tm, LANES), row),
                  pl.BlockSpec((tm, LANES), row)],
        out_specs=(pl.BlockSpec((tm, MLA_HEADS * MLA_QK_PAD), row),
                   pl.BlockSpec((tm, MLA_HEADS * MLA_QK_PAD), row),
                   pl.BlockSpec((tm, MLA_HEADS * V_HEAD), row)),
        compiler_params=_params("parallel"),
        name="mla_proj",
    )(h, g, wd, qn, kvn, wuq, wuk, wuv, cos_t, sin_a, sin_b)


def _mla_attn_kernel(q_ref, k_ref, v_ref, o_ref, s_ref, p_ref, m_ref, l_ref, alpha_ref, acc_ref, *, tq, tk, heads):
    i = pl.program_id(2)
    qg = [q_ref[0, :, g * MLA_QK_PAD:(g + 1) * MLA_QK_PAD] for g in range(heads)]
    rc = MLA_ROW_CHUNK

    m_ref[...] = jnp.full(m_ref.shape, -jnp.inf, F32)
    l_ref[...] = jnp.zeros(l_ref.shape, F32)
    acc_ref[...] = jnp.zeros(acc_ref.shape, F32)

    def scores(g, j):
        kb = k_ref[0, pl.ds(_aligned(j * tk, tk), tk), g * MLA_QK_PAD:(g + 1) * MLA_QK_PAD]
        s_ref[g] = _dot_nt(qg[g], kb)

    def update(g, j, diagonal):
        vb = v_ref[0, pl.ds(_aligned(j * tk, tk), tk), g * V_HEAD:(g + 1) * V_HEAD]
        for c in range(tq // rc):
            rows = slice(c * rc, (c + 1) * rc)
            s = s_ref[g, rows, :]
            if diagonal:
                qc = (lax.broadcasted_iota(jnp.int32, (rc, tk), 0) + c * rc) // CHUNK
                kc = lax.broadcasted_iota(jnp.int32, (rc, tk), 1) // CHUNK
                s = jnp.where(kc <= qc, s, -jnp.inf)
            m_old = m_ref[g, rows, :]
            m_new = jnp.maximum(m_old, jnp.max(s, axis=-1, keepdims=True))
            alpha = jnp.exp2(m_old - m_new)
            m_ref[g, rows, :] = m_new
            alpha_ref[g, rows, :] = alpha
            psum = jnp.zeros((rc, LANES), F32)
            for t in range(tk // LANES):
                p = jnp.exp2(s[:, t * LANES:(t + 1) * LANES] - m_new)
                psum = psum + p
                p_ref[g, rows, t * LANES:(t + 1) * LANES] = p.astype(BF16)
            l_ref[g, rows, :] = alpha * l_ref[g, rows, :] + jnp.sum(psum, axis=-1, keepdims=True)
        acc_ref[g] = alpha_ref[g] * acc_ref[g] + _dot(p_ref[g], vb)

    for g in range(heads):
        scores(g, 0)

    def body(j, carry):
        for g in range(heads):
            update(g, j, False)
            scores(g, j + 1)
        return carry

    lax.fori_loop(0, i, body, 0)
    for g in range(heads):
        update(g, i, True)
    for g in range(heads):
        o_ref[0, :, g * V_HEAD:(g + 1) * V_HEAD] = (acc_ref[g] / l_ref[g]).astype(BF16)


def _mla_attn(q, k, v):
    b, s, _ = q.shape
    tq, tk, hg = TQ_MLA, TK_MLA, MLA_HEADS_PER_STEP
    return pl.pallas_call(
        functools.partial(_mla_attn_kernel, tq=tq, tk=tk, heads=hg),
        out_shape=jax.ShapeDtypeStruct((b, s, MLA_HEADS * V_HEAD), BF16),
        grid=(b, MLA_HEADS // hg, s // tq),
        in_specs=[pl.BlockSpec((1, tq, hg * MLA_QK_PAD), lambda bi, h, i: (bi, i, h)),
                  pl.BlockSpec((1, s, hg * MLA_QK_PAD), lambda bi, h, i: (bi, 0, h)),
                  pl.BlockSpec((1, s, hg * V_HEAD), lambda bi, h, i: (bi, 0, h))],
        out_specs=pl.BlockSpec((1, tq, hg * V_HEAD), lambda bi, h, i: (bi, i, h)),
        scratch_shapes=[pltpu.VMEM((hg, tq, tk), F32), pltpu.VMEM((hg, tq, tk), BF16),
                        pltpu.VMEM((hg, tq, LANES), F32), pltpu.VMEM((hg, tq, LANES), F32),
                        pltpu.VMEM((hg, tq, LANES), F32), pltpu.VMEM((hg, tq, V_HEAD), F32)],
        compiler_params=_params("parallel", "parallel", "arbitrary"),
        name="mla_attn",
    )(q, k, v)


def _cast_kernel(*refs, n_in, n_out):
    per = n_in // n_out
    for o in range(n_out):
        parts = [refs[o * per + k][...].astype(BF16) for k in range(per)]
        refs[n_in + o][...] = parts[0] if per == 1 else jnp.concatenate(parts, axis=1)


def _to_bf16_halves(w, n_in):
    n_out = 2
    cols = w.shape[-1]
    rows = w.size // cols
    wi, wo = cols // n_in, cols // n_out
    tr = max(r for r in range(16, rows + 1, 16) if rows % r == 0 and r * cols * 4 <= CAST_BLOCK_BYTES)
    outs = pl.pallas_call(
        functools.partial(_cast_kernel, n_in=n_in, n_out=n_out),
        out_shape=tuple(jax.ShapeDtypeStruct((rows, wo), BF16) for _ in range(n_out)),
        grid=(rows // tr,),
        in_specs=[pl.BlockSpec((tr, wi), functools.partial(lambda c, i: (i, c), c)) for c in range(n_in)],
        out_specs=tuple(pl.BlockSpec((tr, wo), lambda i: (i, 0)) for _ in range(n_out)),
        compiler_params=_params("parallel"),
        name="cast_bf16",
    )(*([w.reshape(rows, cols)] * n_in))
    return tuple(o.reshape(w.shape[:-1] + (wo,)) for o in outs)


def _norm_matmul_kernel(h_ref, g_ref, w_ref, o_ref, *, scale):
    xn = _rms(h_ref[...], g_ref[...]).astype(BF16)
    o_ref[...] = (_dot(xn, w_ref[...]) * scale).astype(BF16)


def _norm_matmul(h, g, w, scale=1.0):
    n = h.shape[0]
    tm = TM_PROJ
    f = w.shape[1]
    row = lambda i: (i, 0)
    return pl.pallas_call(
        functools.partial(_norm_matmul_kernel, scale=float(scale)),
        out_shape=jax.ShapeDtypeStruct((n, f), BF16),
        grid=(n // tm,),
        in_specs=[pl.BlockSpec((tm, D_MODEL), row),
                  pl.BlockSpec((1, D_MODEL), lambda i: (0, 0)),
                  pl.BlockSpec(w.shape, lambda i: (0, 0))],
        out_specs=pl.BlockSpec((tm, f), row),
        compiler_params=_params("parallel"),
        name="norm_matmul",
    )(h, g, w)


def _swiglu(x, w_refs):
    wg_ref, wu_ref, wda_ref, wdb_ref = w_refs
    gate = _dot(x, wg_ref[...])
    up = _dot(x, wu_ref[...])
    act = (_silu(gate) * up).astype(BF16)
    return jnp.concatenate([_dot(act, wda_ref[...]), _dot(act, wdb_ref[...])], axis=1)


def _ffn_kernel(h_ref, a_ref, wo_ref, g_ref, wg_ref, wu_ref, wda_ref, wdb_ref, o_ref):
    h = h_ref[...] + _dot(a_ref[...], wo_ref[...])
    x = _rms(h, g_ref[...]).astype(BF16)
    o_ref[...] = h + _swiglu(x, (wg_ref, wu_ref, wda_ref, wdb_ref))


def _resident(shape):
    return pl.BlockSpec(shape, lambda *_: (0,) * len(shape), pipeline_mode=pl.Buffered(1))


def _dense_ffn(h, a, w_o, g, ffn_w):
    n = h.shape[0]
    tm = TM_FFN
    row = lambda i: (i, 0)
    return pl.pallas_call(
        _ffn_kernel,
        out_shape=jax.ShapeDtypeStruct(h.shape, F32),
        grid=(n // tm,),
        in_specs=[pl.BlockSpec((tm, D_MODEL), row),
                  pl.BlockSpec((tm, D_MODEL), row),
                  _resident(w_o.shape),
                  _resident(g.shape)] + [_resident(w.shape) for w in ffn_w],
        out_specs=pl.BlockSpec((tm, D_MODEL), row),
        input_output_aliases={0: 0},
        compiler_params=_params("parallel"),
        name="dense_ffn",
    )(h, a, w_o, g, *ffn_w)


def _router_kernel(h_ref, a_ref, wo_ref, g_ref, rhi_ref, rlo_ref, tri_ref,
                   h_out_ref, xn_ref, idx_ref, gate_ref, rank_ref, count_ref, base_ref):
    h = h_ref[...] + _dot(a_ref[...], wo_ref[...])
    h_out_ref[...] = h
    xn = _rms(h, g_ref[...])
    x_hi = xn.astype(BF16)
    x_lo = (xn - x_hi.astype(F32)).astype(BF16)
    xn_ref[...] = x_hi
    logits = _dot_nt(rhi_ref[...], x_hi) + _dot_nt(rhi_ref[...], x_lo) + _dot_nt(rlo_ref[...], x_hi)
    m = jnp.max(logits, axis=0, keepdims=True)
    e = jnp.exp(logits - m)
    p = e / jnp.sum(e, axis=0, keepdims=True)
    eid = lax.broadcasted_iota(jnp.int32, p.shape, 0)
    p1 = jnp.max(p, axis=0, keepdims=True)
    i1 = jnp.min(jnp.where(p == p1, eid, N_EXPERTS), axis=0, keepdims=True)
    pm = jnp.where(eid == i1, -1.0, p)
    p2 = jnp.max(pm, axis=0, keepdims=True)
    i2 = jnp.min(jnp.where(pm == p2, eid, N_EXPERTS), axis=0, keepdims=True)
    den = p1 + p2
    idx_ref[...] = jnp.concatenate([i1, i2], axis=0)
    gate_ref[...] = jnp.concatenate([p1 / den, p2 / den], axis=0)

    @pl.when(pl.program_id(0) == 0)
    def _():
        base_ref[...] = jnp.zeros(base_ref.shape, F32)

    first = (eid == i1).astype(F32)
    second = (eid == i2).astype(F32)
    earlier = tri_ref[...]
    before_first = _dot(first.astype(BF16), earlier)
    before_second = _dot(second.astype(BF16), earlier)
    n_first = jnp.sum(first, axis=1, keepdims=True)
    n_second = jnp.sum(second, axis=1, keepdims=True)
    base = base_ref[:, 0:1]
    rank_first = jnp.sum(first * (base + before_first), axis=0, keepdims=True)
    rank_second = jnp.sum(second * (base + n_first + before_second), axis=0, keepdims=True)
    rank_ref[...] = jnp.concatenate([rank_first, rank_second], axis=0).astype(jnp.int32)
    total = jnp.broadcast_to(base + n_first + n_second, base_ref.shape)
    base_ref[...] = total
    count_ref[...] = total.astype(jnp.int32)


def _router(h, a, w_o, g, r_hi, r_lo):
    n = h.shape[0]
    tm = TM_ROUTER
    row = lambda i: (i, 0)
    col = lambda i: (0, i)
    earlier = (lax.broadcasted_iota(jnp.int32, (tm, tm), 0)
               < lax.broadcasted_iota(jnp.int32, (tm, tm), 1)).astype(BF16)
    return pl.pallas_call(
        _router_kernel,
        out_shape=(jax.ShapeDtypeStruct((n, D_MODEL), F32),
                   jax.ShapeDtypeStruct((n, D_MODEL), BF16),
                   jax.ShapeDtypeStruct((TOP_K, n), jnp.int32),
                   jax.ShapeDtypeStruct((TOP_K, n), F32),
                   jax.ShapeDtypeStruct((TOP_K, n), jnp.int32),
                   jax.ShapeDtypeStruct((N_EXPERTS, LANES), jnp.int32)),
        grid=(n // tm,),
        in_specs=[pl.BlockSpec((tm, D_MODEL), row),
                  pl.BlockSpec((tm, D_MODEL), row),
                  _resident(w_o.shape),
                  _resident(g.shape),
                  _resident(r_hi.shape),
                  _resident(r_lo.shape),
                  _resident(earlier.shape)],
        out_specs=(pl.BlockSpec((tm, D_MODEL), row),
                   pl.BlockSpec((tm, D_MODEL), row),
                   pl.BlockSpec((TOP_K, tm), col),
                   pl.BlockSpec((TOP_K, tm), col),
                   pl.BlockSpec((TOP_K, tm), col),
                   pl.BlockSpec((N_EXPERTS, LANES), lambda i: (0, 0))),
        scratch_shapes=[pltpu.VMEM((N_EXPERTS, LANES), F32)],
        input_output_aliases={0: 0},
        compiler_params=_params("arbitrary"),
        name="moe_router",
    )(h, a, w_o, g, r_hi, r_lo, earlier)


def _moe_kernel(te_ref, nu_ref, x_ref, wg_ref, wu_ref, wda_ref, wdb_ref, o_ref):
    used = pl.program_id(0) < nu_ref[0]

    @pl.when(used)
    def _():
        o_ref[...] = _swiglu(x_ref[...], (wg_ref.at[0], wu_ref.at[0], wda_ref.at[0], wdb_ref.at[0])).astype(BF16)

    @pl.when(jnp.logical_not(used))
    def _():
        o_ref[...] = jnp.zeros(o_ref.shape, BF16)


def _moe_experts(tile_expert, n_used, x_sorted, ffn_w):
    p = x_sorted.shape[0]
    tm = TM_MOE
    grid_spec = pltpu.PrefetchScalarGridSpec(
        num_scalar_prefetch=2,
        grid=(p // tm,),
        in_specs=[pl.BlockSpec((tm, D_MODEL), lambda t, te, nu: (t, 0))]
                 + [pl.BlockSpec((1,) + w.shape[1:], lambda t, te, nu: (te[t], 0, 0)) for w in ffn_w],
        out_specs=pl.BlockSpec((tm, D_MODEL), lambda t, te, nu: (t, 0)),
    )
    return pl.pallas_call(
        _moe_kernel,
        out_shape=jax.ShapeDtypeStruct((p, D_MODEL), BF16),
        grid_spec=grid_spec,
        compiler_params=_params("arbitrary"),
        name="moe_experts",
    )(tile_expert, n_used, x_sorted, *ffn_w)


def _moe_dispatch(h, a, w_o, g, router, ffn_w):
    n = h.shape[0]
    tm, tr = TM_MOE, TM_ROUTER
    r_t = router.T
    r_hi = r_t.astype(BF16)
    r_lo = (r_t - r_hi.astype(F32)).astype(BF16)
    h, xn, idx, gates, rank, count = _router(h, a, w_o, g, r_hi, r_lo)

    experts = jnp.arange(N_EXPERTS, dtype=jnp.int32)
    counts = count[:, 0]
    padded = ((counts + tm - 1) // tm) * tm
    ends = jnp.cumsum(padded)
    offsets = ends - padded
    starts = jnp.cumsum(counts) - counts
    pos = rank + jnp.sum(jnp.where(idx[None] == experts[:, None, None], offsets[:, None, None], 0), axis=0)
    p_rows = TOP_K * n + N_EXPERTS * tm
    n_tiles = p_rows // tm
    n_used = (ends[-1] // tm).astype(jnp.int32)
    tile_start = jnp.arange(n_tiles, dtype=jnp.int32) * tm
    tile_expert = jnp.sum((tile_start[:, None] >= ends[None, :]).astype(jnp.int32), axis=1)
    tile_expert = jnp.minimum(tile_expert, N_EXPERTS - 1)
    last_expert = tile_expert[jnp.maximum(n_used - 1, 0)]
    tile_expert = jnp.where(jnp.arange(n_tiles) < n_used, tile_expert, last_expert)
    e_flat = idx.reshape(TOP_K, n // tr, tr).transpose(1, 0, 2).reshape(-1)
    order = jnp.argsort(e_flat, stable=True).astype(jnp.int32)
    order_tok = (order // (TOP_K * tr)) * tr + order % tr
    within = jnp.arange(p_rows, dtype=jnp.int32) - jnp.repeat(offsets[tile_expert], tm)
    src = jnp.repeat(starts[tile_expert], tm) + within
    valid = within < jnp.repeat(counts[tile_expert], tm)
    tok_sorted = jnp.where(valid, order_tok[jnp.clip(src, 0, TOP_K * n - 1)], 0)
    x_sorted = jnp.take(xn, tok_sorted, axis=0)

    y_sorted = _moe_experts(tile_expert, n_used.reshape(1), x_sorted, ffn_w)
    return h, jnp.take(y_sorted, pos[0], axis=0), jnp.take(y_sorted, pos[1], axis=0), gates.T


def _mix(h_ref, y0_ref, y1_ref, gt_ref):
    gt = gt_ref[...]
    return h_ref[...] + gt[:, 0:1] * y0_ref[...].astype(F32) + gt[:, 1:2] * y1_ref[...].astype(F32)


def _combine_proj_kernel(h_ref, y0_ref, y1_ref, gt_ref, gk_ref, gq_ref, wkv_ref, wq_ref,
                         h_out_ref, kv_ref, q_ref, *, q_scale):
    h = _mix(h_ref, y0_ref, y1_ref, gt_ref)
    h_out_ref[...] = h
    xhat = h * lax.rsqrt(jnp.mean(h * h, axis=-1, keepdims=True) + EPS)
    kv_ref[...] = _dot((xhat * gk_ref[...]).astype(BF16), wkv_ref[...]).astype(BF16)
    q_ref[...] = (_dot((xhat * gq_ref[...]).astype(BF16), wq_ref[...]) * q_scale).astype(BF16)


def _combine_proj(h, y0, y1, gt, g_kv, g_q, w_kv, w_q, q_scale):
    n = h.shape[0]
    tm = TM_PROJ
    row = lambda i: (i, 0)
    return pl.pallas_call(
        functools.partial(_combine_proj_kernel, q_scale=float(q_scale)),
        out_shape=(jax.ShapeDtypeStruct(h.shape, F32),
                   jax.ShapeDtypeStruct((n, w_kv.shape[1]), BF16),
                   jax.ShapeDtypeStruct((n, w_q.shape[1]), BF16)),
        grid=(n // tm,),
        in_specs=[pl.BlockSpec((tm, D_MODEL), row),
                  pl.BlockSpec((tm, D_MODEL), row),
                  pl.BlockSpec((tm, D_MODEL), row),
                  pl.BlockSpec((tm, TOP_K), row),
                  _resident(g_kv.shape),
                  _resident(g_q.shape),
                  _resident(w_kv.shape),
                  _resident(w_q.shape)],
        out_specs=(pl.BlockSpec((tm, D_MODEL), row),
                   pl.BlockSpec((tm, w_kv.shape[1]), row),
                   pl.BlockSpec((tm, w_q.shape[1]), row)),
        input_output_aliases={0: 0},
        compiler_params=_params("parallel"),
        name="combine_proj",
    )(h, y0, y1, gt, g_kv, g_q, w_kv, w_q)


def _combine_norm_kernel(h_ref, y0_ref, y1_ref, gt_ref, g_ref, o_ref):
    o_ref[...] = _rms(_mix(h_ref, y0_ref, y1_ref, gt_ref), g_ref[...])


def _combine_norm(h, y0, y1, gt, g):
    n = h.shape[0]
    tm = TM_PROJ
    row = lambda i: (i, 0)
    return pl.pallas_call(
        _combine_norm_kernel,
        out_shape=jax.ShapeDtypeStruct(h.shape, F32),
        grid=(n // tm,),
        in_specs=[pl.BlockSpec((tm, D_MODEL), row),
                  pl.BlockSpec((tm, D_MODEL), row),
                  pl.BlockSpec((tm, D_MODEL), row),
                  pl.BlockSpec((tm, TOP_K), row),
                  _resident(g.shape)],
        out_specs=pl.BlockSpec((tm, D_MODEL), row),
        input_output_aliases={0: 0},
        compiler_params=_params("parallel"),
        name="combine_norm",
    )(h, y0, y1, gt, g)


def _sb_attn_kernel(q_ref, k_ref, v_ref, u_ref, o_ref, *, tb, heads):
    tri = u_ref[...]
    nq = q_ref.shape[1] // tb
    below_diag = (lax.broadcasted_iota(jnp.int32, (tb, tb), 1)
                  < lax.broadcasted_iota(jnp.int32, (tb, tb), 0))

    def block(q, g, jb, r_sum, acc, diagonal):
        start = _aligned(jb * tb, tb)
        kb = k_ref[0, pl.ds(start, tb), g * SB_HEAD:(g + 1) * SB_HEAD]
        vb = v_ref[0, pl.ds(start, tb), g * SB_HEAD:(g + 1) * SB_HEAD]
        z = _dot_nt(q, kb)
        neg_abs = pltpu.bitcast(pltpu.bitcast(z, jnp.uint32) | jnp.uint32(0x80000000), F32)
        soft = jnp.log(1.0 + jnp.exp2(neg_abs)) * LOG2E
        ls = jnp.minimum(z, 0.0) - soft
        lb = ls - z
        if diagonal:
            lb = jnp.where(below_diag, lb, 0.0)
        between = _dot(lb.astype(BF16), tri)
        a = jnp.exp2(ls + between + r_sum)
        if diagonal:
            a = jnp.where(below_diag, a, 0.0)
        acc = acc + _dot(a.astype(BF16), vb)
        r_sum = r_sum + jnp.sum(lb, axis=-1, keepdims=True)
        return r_sum, acc

    def q_block(i, first):
        qs = _aligned(i * tb, tb)
        qg = [q_ref[0, pl.ds(qs, tb), g * SB_HEAD:(g + 1) * SB_HEAD] for g in range(heads)]
        zero_r, zero_acc = jnp.zeros((tb, 1), F32), jnp.zeros((tb, SB_HEAD), F32)
        state = [block(qg[g], g, i, zero_r, zero_acc, True) for g in range(heads)]
        if not first:
            state = [block(qg[g], g, i - 1, state[g][0], state[g][1], False) for g in range(heads)]

            def cond(c):
                live = jnp.max(c[1][0][0])
                for g in range(1, heads):
                    live = jnp.maximum(live, jnp.max(c[1][g][0]))
                return jnp.logical_and(c[0] >= 0, live > SB_EXIT_LOG2)

            def body(c):
                return c[0] - 1, [block(qg[g], g, c[0], c[1][g][0], c[1][g][1], False) for g in range(heads)]

            _, state = lax.while_loop(cond, body, (i - 2, state))
        for g in range(heads):
            o_ref[0, pl.ds(qs, tb), g * SB_HEAD:(g + 1) * SB_HEAD] = state[g][1].astype(BF16)

    q_block(0, True)

    def rest(i, carry):
        q_block(i, False)
        return carry

    lax.fori_loop(1, nq, rest, 0)


def _sb_attn(q, kv, tri):
    b, s, _ = q.shape
    hg = SB_HEADS_PER_STEP
    ng = SB_HEADS // hg
    w = hg * SB_HEAD
    return pl.pallas_call(
        functools.partial(_sb_attn_kernel, tb=TB_SB, heads=hg),
        out_shape=jax.ShapeDtypeStruct((b, s, SB_HEADS * SB_HEAD), BF16),
        grid=(b, ng),
        in_specs=[pl.BlockSpec((1, s, w), lambda bi, h: (bi, 0, h)),
                  pl.BlockSpec((1, s, w), lambda bi, h: (bi, 0, h)),
                  pl.BlockSpec((1, s, w), lambda bi, h: (bi, 0, ng + h)),
                  pl.BlockSpec((TB_SB, TB_SB), lambda bi, h: (0, 0))],
        out_specs=pl.BlockSpec((1, s, w), lambda bi, h: (bi, 0, h)),
        compiler_params=_params("parallel", "parallel"),
        name="sb_attn",
    )(q, kv, kv, tri)


def _rope_tables(positions):
    inv_freq = 1.0 / (ROPE_BASE ** (jnp.arange(0, QK_ROPE, 2, dtype=F32) / QK_ROPE))
    ang = positions.astype(F32).reshape(-1, 1) * inv_freq
    cos, sin = jnp.cos(ang), jnp.sin(ang)
    z32 = jnp.zeros_like(cos)
    z64 = jnp.zeros((cos.shape[0], LANES - QK_ROPE), F32)
    cos_t = jnp.concatenate([cos, cos, z64], axis=-1)
    sin_a = jnp.concatenate([-sin, z32, z64], axis=-1)
    sin_b = jnp.concatenate([z32, sin, z64], axis=-1)
    return cos_t, sin_a, sin_b


def _mla_weights(w_down, w_uq, w_ukv):
    wd = jnp.pad(w_down, ((0, 0), (0, DOWN_PAD - w_down.shape[1]))).astype(BF16)
    wuq = w_uq.reshape(Q_LORA, MLA_HEADS, QK_NOPE + QK_ROPE)
    wuq = jnp.pad(wuq, ((0, 0), (0, 0), (0, MLA_QK_PAD - QK_NOPE - QK_ROPE)))
    wuq = wuq.reshape(Q_LORA, MLA_HEADS * MLA_QK_PAD).astype(BF16)
    wukv = w_ukv.reshape(KV_LORA, MLA_HEADS, QK_NOPE + V_HEAD)
    wuk = wukv[:, :, :QK_NOPE].reshape(KV_LORA, MLA_HEADS * QK_NOPE).astype(BF16)
    wuv = wukv[:, :, QK_NOPE:].reshape(KV_LORA, MLA_HEADS * V_HEAD).astype(BF16)
    return wd, wuq, wuk, wuv


def kernel(x, positions, attn_norm_0, mla_w_down_0, mla_q_norm_0, mla_w_uq_0, mla_kv_norm_0, mla_w_ukv_0, mla_w_o_0, ffn_norm_0, ffn_w_gate_up_0, ffn_w_down_0, attn_norm_1, mla_w_down_1, mla_q_norm_1, mla_w_uq_1, mla_kv_norm_1, mla_w_ukv_1, mla_w_o_1, ffn_norm_1, moe_router_1, moe_w_gate_up_1, moe_w_down_1, kv_shared_norm, kv_shared_w, attn_norm_2, sb_w_q_2, sb_w_o_2, ffn_norm_2, ffn_w_gate_up_2, ffn_w_down_2, attn_norm_3, sb_w_q_3, sb_w_o_3, ffn_norm_3, moe_router_3, moe_w_gate_up_3, moe_w_down_3, final_norm):
    b, s, d = x.shape
    n = b * s
    row = lambda g: g.reshape(1, -1)
    h = x.reshape(n, d)
    cos_t, sin_a, sin_b = _rope_tables(positions)

    mla_layers = [
        (attn_norm_0, mla_w_down_0, mla_q_norm_0, mla_w_uq_0, mla_kv_norm_0, mla_w_ukv_0, mla_w_o_0),
        (attn_norm_1, mla_w_down_1, mla_q_norm_1, mla_w_uq_1, mla_kv_norm_1, mla_w_ukv_1, mla_w_o_1),
    ]
    tri = (lax.broadcasted_iota(jnp.int32, (TB_SB, TB_SB), 0)
           > lax.broadcasted_iota(jnp.int32, (TB_SB, TB_SB), 1)).astype(BF16)
    sb_scale = SB_HEAD ** -0.5 * LOG2E

    def ffn_weights(w_gate_up, w_down):
        return _to_bf16_halves(w_gate_up, 4) + _to_bf16_halves(w_down, 2)

    def mla_attention(h, layer):
        an, w_down, qn, w_uq, kvn, w_ukv, _ = mla_layers[layer]
        wd, wuq, wuk, wuv = _mla_weights(w_down, w_uq, w_ukv)
        q, k, v = _mla_proj(h, row(an), wd, row(qn), row(kvn), wuq, wuk, wuv, cos_t, sin_a, sin_b)
        return _mla_attn(q.reshape(b, s, -1), k.reshape(b, s, -1), v.reshape(b, s, -1)).reshape(n, -1)

    o = mla_attention(h, 0)
    h = _dense_ffn(h, o, mla_w_o_0.astype(BF16), row(ffn_norm_0), ffn_weights(ffn_w_gate_up_0, ffn_w_down_0))
    o = mla_attention(h, 1)
    h, y0, y1, gt = _moe_dispatch(h, o, mla_w_o_1.astype(BF16), row(ffn_norm_1), moe_router_1,
                                  ffn_weights(moe_w_gate_up_1, moe_w_down_1))
    h, kv_shared, q = _combine_proj(h, y0, y1, gt, row(kv_shared_norm), row(attn_norm_2),
                                    kv_shared_w.astype(BF16), sb_w_q_2.astype(BF16), sb_scale)
    kv_shared = kv_shared.reshape(b, s, -1)
    o = _sb_attn(q.reshape(b, s, -1), kv_shared, tri).reshape(n, -1)
    h = _dense_ffn(h, o, sb_w_o_2.astype(BF16), row(ffn_norm_2), ffn_weights(ffn_w_gate_up_2, ffn_w_down_2))
    q = _norm_matmul(h, row(attn_norm_3), sb_w_q_3.astype(BF16), scale=sb_scale)
    o = _sb_attn(q.reshape(b, s, -1), kv_shared, tri).reshape(n, -1)
    h, y0, y1, gt = _moe_dispatch(h, o, sb_w_o_3.astype(BF16), row(ffn_norm_3), moe_router_3,
                                  ffn_weights(moe_w_gate_up_3, moe_w_down_3))
    return _combine_norm(h, y0, y1, gt, row(final_norm)).reshape(b, s, d)
```

```python
import functools

import jax
import jax.numpy as jnp
from jax import lax
from jax.experimental import pallas as pl
from jax.experimental.pallas import tpu as pltpu

F32 = jnp.float32
BF16 = jnp.bfloat16

D_MODEL = 1024
CHUNK = 64
MLA_HEADS = 8
QK_NOPE = 128
QK_ROPE = 64
V_HEAD = 128
Q_LORA = 384
KV_LORA = 256
ROPE_BASE = 10000.0
SB_HEADS = 8
SB_HEAD = 128
D_FF = 2816
N_EXPERTS = 8
TOP_K = 2
EPS = 1e-6

LANES = 128
MLA_QK_PAD = 256
DOWN_PAD = 768
VMEM_LIMIT = 56 * 1024 * 1024

TM_PROJ = 512
TM_FFN = 512
TM_MOE = 256
CAST_BLOCK_BYTES = 12 * 1024 * 1024
TQ_MLA = 512
TK_MLA = 512
MLA_ROW_CHUNK = 32
MLA_HEADS_PER_STEP = 4
TB_SB = 256
SB_HEADS_PER_STEP = 4
TM_ROUTER = 512
LOG2E = 1.4426950408889634
SB_EXIT_LOG2 = -150.0


def _rms(x, g):
    ms = jnp.mean(x * x, axis=-1, keepdims=True)
    return x * lax.rsqrt(ms + EPS) * g


def _dot(a, b):
    return jnp.dot(a, b, preferred_element_type=F32)


def _dot_nt(a, b):
    return lax.dot_general(a, b, (((1,), (1,)), ((), ())), preferred_element_type=F32)


def _silu(g):
    return g / (1.0 + jnp.exp(-g))


def _aligned(x, m):
    return x if isinstance(x, int) else pl.multiple_of(x, m)


def _params(*sem):
    return pltpu.CompilerParams(dimension_semantics=sem, vmem_limit_bytes=VMEM_LIMIT)


def _mla_proj_kernel(h_ref, g_ref, wd_ref, qn_ref, kvn_ref, wuq_ref, wuk_ref, wuv_ref,
                     c_ref, sa_ref, sb_ref, q_ref, k_ref, v_ref, *, scale):
    xn = _rms(h_ref[...], g_ref[...]).astype(BF16)
    c = _dot(xn, wd_ref[...])
    cq = _rms(c[:, :Q_LORA], qn_ref[...]).astype(BF16)
    ckv = _rms(c[:, Q_LORA:Q_LORA + KV_LORA], kvn_ref[...]).astype(BF16)
    cos_t, sin_a, sin_b = c_ref[...], sa_ref[...], sb_ref[...]

    def rope(r):
        return r * cos_t + pltpu.roll(r, 96, 1) * sin_a + pltpu.roll(r, 32, 1) * sin_b

    kr = rope(c[:, Q_LORA + KV_LORA:]).astype(BF16)
    q = _dot(cq, wuq_ref[...])
    kn = _dot(ckv, wuk_ref[...])
    v_ref[...] = _dot(ckv, wuv_ref[...]).astype(BF16)
    for h in range(MLA_HEADS):
        lo = h * MLA_QK_PAD
        q_ref[:, lo:lo + LANES] = (q[:, lo:lo + LANES] * scale).astype(BF16)
        q_ref[:, lo + LANES:lo + 2 * LANES] = (rope(q[:, lo + LANES:lo + 2 * LANES]) * scale).astype(BF16)
        k_ref[:, lo:lo + LANES] = kn[:, h * LANES:(h + 1) * LANES].astype(BF16)
        k_ref[:, lo + LANES:lo + 2 * LANES] = kr


def _mla_proj(h, g, wd, qn, kvn, wuq, wuk, wuv, cos_t, sin_a, sin_b):
    n = h.shape[0]
    tm = TM_PROJ
    row = lambda i: (i, 0)
    fixed = lambda i: (0, 0)
    scale = float((QK_NOPE + QK_ROPE) ** -0.5 * LOG2E)
    return pl.pallas_call(
        functools.partial(_mla_proj_kernel, scale=scale),
        out_shape=(jax.ShapeDtypeStruct((n, MLA_HEADS * MLA_QK_PAD), BF16),
                   jax.ShapeDtypeStruct((n, MLA_HEADS * MLA_QK_PAD), BF16),
                   jax.ShapeDtypeStruct((n, MLA_HEADS * V_HEAD), BF16)),
        grid=(n // tm,),
        in_specs=[pl.BlockSpec((tm, D_MODEL), row),
                  pl.BlockSpec((1, D_MODEL), fixed),
                  pl.BlockSpec((D_MODEL, DOWN_PAD), fixed),
                  pl.BlockSpec((1, Q_LORA), fixed),
                  pl.BlockSpec((1, KV_LORA), fixed),
                  pl.BlockSpec((Q_LORA, MLA_HEADS * MLA_QK_PAD), fixed),
                  pl.BlockSpec((KV_LORA, MLA_HEADS * QK_NOPE), fixed),
                  pl.BlockSpec((KV_LORA, MLA_HEADS * V_HEAD), fixed),
                  pl.BlockSpec((tm, LANES), row),
                  pl.BlockSpec((tm, LANES), row),
                  pl.BlockSpec((tm, LANES), row)],
        out_specs=(pl.BlockSpec((tm, MLA_HEADS * MLA_QK_PAD), row),
                   pl.BlockSpec((tm, MLA_HEADS * MLA_QK_PAD), row),
                   pl.BlockSpec((tm, MLA_HEADS * V_HEAD), row)),
        compiler_params=_params("parallel"),
        name="mla_proj",
    )(h, g, wd, qn, kvn, wuq, wuk, wuv, cos_t, sin_a, sin_b)


def _mla_attn_kernel(q_ref, k_ref, v_ref, wgu_ref, wd_ref, o_ref, wg_ref, wu_ref, wda_ref, wdb_ref,
                     s_ref, p_ref, m_ref, l_ref, alpha_ref, acc_ref, *, tq, tk, heads):
    wg_ref[...] = wgu_ref[:, :D_FF].astype(BF16)
    wu_ref[...] = wgu_ref[:, D_FF:].astype(BF16)
    wda_ref[...] = wd_ref[:, :D_MODEL // 2].astype(BF16)
    wdb_ref[...] = wd_ref[:, D_MODEL // 2:].astype(BF16)

    i = pl.program_id(2)
    qg = [q_ref[0, :, g * MLA_QK_PAD:(g + 1) * MLA_QK_PAD] for g in range(heads)]
    rc = MLA_ROW_CHUNK

    m_ref[...] = jnp.full(m_ref.shape, -jnp.inf, F32)
    l_ref[...] = jnp.zeros(l_ref.shape, F32)
    acc_ref[...] = jnp.zeros(acc_ref.shape, F32)

    def scores(g, j):
        kb = k_ref[0, pl.ds(_aligned(j * tk, tk), tk), g * MLA_QK_PAD:(g + 1) * MLA_QK_PAD]
        s_ref[g] = _dot_nt(qg[g], kb)

    def update(g, j, diagonal):
        vb = v_ref[0, pl.ds(_aligned(j * tk, tk), tk), g * V_HEAD:(g + 1) * V_HEAD]
        for c in range(tq // rc):
            rows = slice(c * rc, (c + 1) * rc)
            s = s_ref[g, rows, :]
            if diagonal:
                qc = (lax.broadcasted_iota(jnp.int32, (rc, tk), 0) + c * rc) // CHUNK
                kc = lax.broadcasted_iota(jnp.int32, (rc, tk), 1) // CHUNK
                s = jnp.where(kc <= qc, s, -jnp.inf)
            m_old = m_ref[g, rows, :]
            m_new = jnp.maximum(m_old, jnp.max(s, axis=-1, keepdims=True))
            alpha = jnp.exp2(m_old - m_new)
            m_ref[g, rows, :] = m_new
            alpha_ref[g, rows, :] = alpha
            psum = jnp.zeros((rc, LANES), F32)
            for t in range(tk // LANES):
                p = jnp.exp2(s[:, t * LANES:(t + 1) * LANES] - m_new)
                psum = psum + p
                p_ref[g, rows, t * LANES:(t + 1) * LANES] = p.astype(BF16)
            l_ref[g, rows, :] = alpha * l_ref[g, rows, :] + jnp.sum(psum, axis=-1, keepdims=True)
        acc_ref[g] = alpha_ref[g] * acc_ref[g] + _dot(p_ref[g], vb)

    for g in range(heads):
        scores(g, 0)

    def body(j, carry):
        for g in range(heads):
            update(g, j, False)
            scores(g, j + 1)
        return carry

    lax.fori_loop(0, i, body, 0)
    for g in range(heads):
        update(g, i, True)
    for g in range(heads):
        o_ref[0, :, g * V_HEAD:(g + 1) * V_HEAD] = (acc_ref[g] / l_ref[g]).astype(BF16)


def _mla_attn(q, k, v, w_gate_up, w_down):
    b, s, _ = q.shape
    tq, tk, hg = TQ_MLA, TK_MLA, MLA_HEADS_PER_STEP
    ng, nq = MLA_HEADS // hg, s // tq
    steps = b * ng * nq
    wgu2 = w_gate_up.reshape(-1, 2 * D_FF)
    wd2 = w_down.reshape(-1, D_MODEL)
    r_gu, r_d = wgu2.shape[0] // steps, wd2.shape[0] // steps
    assert r_gu * steps == wgu2.shape[0] and r_d * steps == wd2.shape[0] and r_gu % 16 == 0 and r_d % 16 == 0
    slab = lambda bi, h, i: ((bi * ng + h) * nq + i, 0)
    o, wg, wu, wda, wdb = pl.pallas_call(
        functools.partial(_mla_attn_kernel, tq=tq, tk=tk, heads=hg),
        out_shape=(jax.ShapeDtypeStruct((b, s, MLA_HEADS * V_HEAD), BF16),
                   jax.ShapeDtypeStruct((wgu2.shape[0], D_FF), BF16),
                   jax.ShapeDtypeStruct((wgu2.shape[0], D_FF), BF16),
                   jax.ShapeDtypeStruct((wd2.shape[0], D_MODEL // 2), BF16),
                   jax.ShapeDtypeStruct((wd2.shape[0], D_MODEL // 2), BF16)),
        grid=(b, ng, nq),
        in_specs=[pl.BlockSpec((1, tq, hg * MLA_QK_PAD), lambda bi, h, i: (bi, i, h)),
                  pl.BlockSpec((1, s, hg * MLA_QK_PAD), lambda bi, h, i: (bi, 0, h)),
                  pl.BlockSpec((1, s, hg * V_HEAD), lambda bi, h, i: (bi, 0, h)),
                  pl.BlockSpec((r_gu, 2 * D_FF), slab),
                  pl.BlockSpec((r_d, D_MODEL), slab)],
        out_specs=(pl.BlockSpec((1, tq, hg * V_HEAD), lambda bi, h, i: (bi, i, h)),
                   pl.BlockSpec((r_gu, D_FF), slab),
                   pl.BlockSpec((r_gu, D_FF), slab),
                   pl.BlockSpec((r_d, D_MODEL // 2), slab),
                   pl.BlockSpec((r_d, D_MODEL // 2), slab)),
        scratch_shapes=[pltpu.VMEM((hg, tq, tk), F32), pltpu.VMEM((hg, tq, tk), BF16),
                        pltpu.VMEM((hg, tq, LANES), F32), pltpu.VMEM((hg, tq, LANES), F32),
                        pltpu.VMEM((hg, tq, LANES), F32), pltpu.VMEM((hg, tq, V_HEAD), F32)],
        compiler_params=_params("parallel", "parallel", "arbitrary"),
        name="mla_attn",
    )(q, k, v, wgu2, wd2)
    e = w_gate_up.shape[0]
    ffn_w = (wg.reshape(e, D_MODEL, D_FF), wu.reshape(e, D_MODEL, D_FF),
             wda.reshape(e, D_FF, D_MODEL // 2), wdb.reshape(e, D_FF, D_MODEL // 2))
    return o, ffn_w


def _cast_kernel(*refs, n_in, n_out):
    per = n_in // n_out
    for o in range(n_out):
        parts = [refs[o * per + k][...].astype(BF16) for k in range(per)]
        refs[n_in + o][...] = parts[0] if per == 1 else jnp.concatenate(parts, axis=1)


def _to_bf16_halves(w, n_in):
    n_out = 2
    cols = w.shape[-1]
    rows = w.size // cols
    wi, wo = cols // n_in, cols // n_out
    tr = max(r for r in range(16, rows + 1, 16) if rows % r == 0 and r * cols * 4 <= CAST_BLOCK_BYTES)
    outs = pl.pallas_call(
        functools.partial(_cast_kernel, n_in=n_in, n_out=n_out),
        out_shape=tuple(jax.ShapeDtypeStruct((rows, wo), BF16) for _ in range(n_out)),
        grid=(rows // tr,),
        in_specs=[pl.BlockSpec((tr, wi), functools.partial(lambda c, i: (i, c), c)) for c in range(n_in)],
        out_specs=tuple(pl.BlockSpec((tr, wo), lambda i: (i, 0)) for _ in range(n_out)),
        compiler_params=_params("parallel"),
        name="cast_bf16",
    )(*([w.reshape(rows, cols)] * n_in))
    return tuple(o.reshape(w.shape[:-1] + (wo,)) for o in outs)


def _norm_matmul_kernel(h_ref, g_ref, w_ref, o_ref, *, scale):
    xn = _rms(h_ref[...], g_ref[...]).astype(BF16)
    o_ref[...] = (_dot(xn, w_ref[...]) * scale).astype(BF16)


def _norm_matmul(h, g, w, scale=1.0):
    n = h.shape[0]
    tm = TM_PROJ
    f = w.shape[1]
    row = lambda i: (i, 0)
    return pl.pallas_call(
        functools.partial(_norm_matmul_kernel, scale=float(scale)),
        out_shape=jax.ShapeDtypeStruct((n, f), BF16),
        grid=(n // tm,),
        in_specs=[pl.BlockSpec((tm, D_MODEL), row),
                  pl.BlockSpec((1, D_MODEL), lambda i: (0, 0)),
                  pl.BlockSpec(w.shape, lambda i: (0, 0))],
        out_specs=pl.BlockSpec((tm, f), row),
        compiler_params=_params("parallel"),
        name="norm_matmul",
    )(h, g, w)


def _swiglu(x, w_refs):
    wg_ref, wu_ref, wda_ref, wdb_ref = w_refs
    gate = _dot(x, wg_ref[...])
    up = _dot(x, wu_ref[...])
    act = (_silu(gate) * up).astype(BF16)
    return jnp.concatenate([_dot(act, wda_ref[...]), _dot(act, wdb_ref[...])], axis=1)


def _ffn_kernel(h_ref, a_ref, wo_ref, g_ref, wg_ref, wu_ref, wda_ref, wdb_ref, o_ref):
    h = h_ref[...] + _dot(a_ref[...], wo_ref[...])
    x = _rms(h, g_ref[...]).astype(BF16)
    o_ref[...] = h + _swiglu(x, (wg_ref, wu_ref, wda_ref, wdb_ref))


def _resident(shape):
    return pl.BlockSpec(shape, lambda *_: (0,) * len(shape), pipeline_mode=pl.Buffered(1))


def _dense_ffn(h, a, w_o, g, ffn_w, in_place):
    n = h.shape[0]
    tm = TM_FFN
    row = lambda i: (i, 0)
    return pl.pallas_call(
        _ffn_kernel,
        out_shape=jax.ShapeDtypeStruct(h.shape, F32),
        grid=(n // tm,),
        in_specs=[pl.BlockSpec((tm, D_MODEL), row),
                  pl.BlockSpec((tm, D_MODEL), row),
                  _resident(w_o.shape),
                  _resident(g.shape)] + [_resident(w.shape) for w in ffn_w],
        out_specs=pl.BlockSpec((tm, D_MODEL), row),
        input_output_aliases={0: 0} if in_place else {},
        compiler_params=_params("parallel"),
        name="dense_ffn",
    )(h, a, w_o, g, *ffn_w)


def _router_kernel(h_ref, a_ref, wo_ref, g_ref, rhi_ref, rlo_ref, tri_ref,
                   h_out_ref, xn_ref, idx_ref, gate_ref, rank_ref, count_ref, base_ref):
    h = h_ref[...] + _dot(a_ref[...], wo_ref[...])
    h_out_ref[...] = h
    xn = _rms(h, g_ref[...])
    x_hi = xn.astype(BF16)
    x_lo = (xn - x_hi.astype(F32)).astype(BF16)
    xn_ref[...] = x_hi
    logits = _dot_nt(rhi_ref[...], x_hi) + _dot_nt(rhi_ref[...], x_lo) + _dot_nt(rlo_ref[...], x_hi)
    m = jnp.max(logits, axis=0, keepdims=True)
    e = jnp.exp(logits - m)
    p = e / jnp.sum(e, axis=0, keepdims=True)
    eid = lax.broadcasted_iota(jnp.int32, p.shape, 0)
    p1 = jnp.max(p, axis=0, keepdims=True)
    i1 = jnp.min(jnp.where(p == p1, eid, N_EXPERTS), axis=0, keepdims=True)
    pm = jnp.where(eid == i1, -1.0, p)
    p2 = jnp.max(pm, axis=0, keepdims=True)
    i2 = jnp.min(jnp.where(pm == p2, eid, N_EXPERTS), axis=0, keepdims=True)
    den = p1 + p2
    idx_ref[...] = jnp.concatenate([i1, i2], axis=0)
    gate_ref[...] = jnp.concatenate([p1 / den, p2 / den], axis=0)

    @pl.when(pl.program_id(0) == 0)
    def _():
        base_ref[...] = jnp.zeros(base_ref.shape, F32)

    first = (eid == i1).astype(F32)
    second = (eid == i2).astype(F32)
    earlier = tri_ref[...]
    before_first = _dot(first.astype(BF16), earlier)
    before_second = _dot(second.astype(BF16), earlier)
    n_first = jnp.sum(first, axis=1, keepdims=True)
    n_second = jnp.sum(second, axis=1, keepdims=True)
    base = base_ref[:, 0:1]
    rank_first = jnp.sum(first * (base + before_first), axis=0, keepdims=True)
    rank_second = jnp.sum(second * (base + n_first + before_second), axis=0, keepdims=True)
    rank_ref[...] = jnp.concatenate([rank_first, rank_second], axis=0).astype(jnp.int32)
    total = jnp.broadcast_to(base + n_first + n_second, base_ref.shape)
    base_ref[...] = total
    count_ref[...] = total.astype(jnp.int32)


def _router(h, a, w_o, g, r_hi, r_lo):
    n = h.shape[0]
    tm = TM_ROUTER
    row = lambda i: (i, 0)
    col = lambda i: (0, i)
    earlier = (lax.broadcasted_iota(jnp.int32, (tm, tm), 0)
               < lax.broadcasted_iota(jnp.int32, (tm, tm), 1)).astype(BF16)
    return pl.pallas_call(
        _router_kernel,
        out_shape=(jax.ShapeDtypeStruct((n, D_MODEL), F32),
                   jax.ShapeDtypeStruct((n, D_MODEL), BF16),
                   jax.ShapeDtypeStruct((TOP_K, n), jnp.int32),
                   jax.ShapeDtypeStruct((TOP_K, n), F32),
                   jax.ShapeDtypeStruct((TOP_K, n), jnp.int32),
                   jax.ShapeDtypeStruct((N_EXPERTS, LANES), jnp.int32)),
        grid=(n // tm,),
        in_specs=[pl.BlockSpec((tm, D_MODEL), row),
                  pl.BlockSpec((tm, D_MODEL), row),
                  _resident(w_o.shape),
                  _resident(g.shape),
                  _resident(r_hi.shape),
                  _resident(r_lo.shape),
                  _resident(earlier.shape)],
        out_specs=(pl.BlockSpec((tm, D_MODEL), row),
                   pl.BlockSpec((tm, D_MODEL), row),
                   pl.BlockSpec((TOP_K, tm), col),
                   pl.BlockSpec((TOP_K, tm), col),
                   pl.BlockSpec((TOP_K, tm), col),
                   pl.BlockSpec((N_EXPERTS, LANES), lambda i: (0, 0))),
        scratch_shapes=[pltpu.VMEM((N_EXPERTS, LANES), F32)],
        input_output_aliases={0: 0},
        compiler_params=_params("arbitrary"),
        name="moe_router",
    )(h, a, w_o, g, r_hi, r_lo, earlier)


def _moe_kernel(te_ref, nu_ref, x_ref, wg_ref, wu_ref, wda_ref, wdb_ref, o_ref):
    used = pl.program_id(0) < nu_ref[0]

    @pl.when(used)
    def _():
        o_ref[...] = _swiglu(x_ref[...], (wg_ref.at[0], wu_ref.at[0], wda_ref.at[0], wdb_ref.at[0])).astype(BF16)

    @pl.when(jnp.logical_not(used))
    def _():
        o_ref[...] = jnp.zeros(o_ref.shape, BF16)


def _moe_experts(tile_expert, n_used, x_sorted, ffn_w):
    p = x_sorted.shape[0]
    tm = TM_MOE
    grid_spec = pltpu.PrefetchScalarGridSpec(
        num_scalar_prefetch=2,
        grid=(p // tm,),
        in_specs=[pl.BlockSpec((tm, D_MODEL), lambda t, te, nu: (t, 0))]
                 + [pl.BlockSpec((1,) + w.shape[1:], lambda t, te, nu: (te[t], 0, 0)) for w in ffn_w],
        out_specs=pl.BlockSpec((tm, D_MODEL), lambda t, te, nu: (t, 0)),
    )
    return pl.pallas_call(
        _moe_kernel,
        out_shape=jax.ShapeDtypeStruct((p, D_MODEL), BF16),
        grid_spec=grid_spec,
        compiler_params=_params("arbitrary"),
        name="moe_experts",
    )(tile_expert, n_used, x_sorted, *ffn_w)


def _moe_dispatch(h, a, w_o, g, router, ffn_w):
    n = h.shape[0]
    tm, tr = TM_MOE, TM_ROUTER
    r_t = router.T
    r_hi = r_t.astype(BF16)
    r_lo = (r_t - r_hi.astype(F32)).astype(BF16)
    h, xn, idx, gates, rank, count = _router(h, a, w_o, g, r_hi, r_lo)

    experts = jnp.arange(N_EXPERTS, dtype=jnp.int32)
    counts = count[:, 0]
    padded = ((counts + tm - 1) // tm) * tm
    ends = jnp.cumsum(padded)
    offsets = ends - padded
    starts = jnp.cumsum(counts) - counts
    pos = rank + jnp.sum(jnp.where(idx[None] == experts[:, None, None], offsets[:, None, None], 0), axis=0)
    p_rows = TOP_K * n + N_EXPERTS * tm
    n_tiles = p_rows // tm
    n_used = (ends[-1] // tm).astype(jnp.int32)
    tile_start = jnp.arange(n_tiles, dtype=jnp.int32) * tm
    tile_expert = jnp.sum((tile_start[:, None] >= ends[None, :]).astype(jnp.int32), axis=1)
    tile_expert = jnp.minimum(tile_expert, N_EXPERTS - 1)
    last_expert = tile_expert[jnp.maximum(n_used - 1, 0)]
    tile_expert = jnp.where(jnp.arange(n_tiles) < n_used, tile_expert, last_expert)
    e_flat = idx.reshape(TOP_K, n // tr, tr).transpose(1, 0, 2).reshape(-1)
    order = jnp.argsort(e_flat, stable=True).astype(jnp.int32)
    order_tok = (order // (TOP_K * tr)) * tr + order % tr
    within = jnp.arange(p_rows, dtype=jnp.int32) - jnp.repeat(offsets[tile_expert], tm)
    src = jnp.repeat(starts[tile_expert], tm) + within
    valid = within < jnp.repeat(counts[tile_expert], tm)
    tok_sorted = jnp.where(valid, order_tok[jnp.clip(src, 0, TOP_K * n - 1)], 0)
    x_sorted = jnp.take(xn, tok_sorted, axis=0, mode="clip")

    y_sorted = _moe_experts(tile_expert, n_used.reshape(1), x_sorted, ffn_w)
    y0 = jnp.take(y_sorted, pos[0], axis=0, mode="clip")
    y1 = jnp.take(y_sorted, pos[1], axis=0, mode="clip")
    return h, y0, y1, gates.T


def _mix(h_ref, y0_ref, y1_ref, gt_ref):
    gt = gt_ref[...]
    return h_ref[...] + gt[:, 0:1] * y0_ref[...].astype(F32) + gt[:, 1:2] * y1_ref[...].astype(F32)


def _combine_proj_kernel(h_ref, y0_ref, y1_ref, gt_ref, gk_ref, gq_ref, wkv_ref, wq_ref,
                         h_out_ref, kv_ref, q_ref, *, q_scale):
    h = _mix(h_ref, y0_ref, y1_ref, gt_ref)
    h_out_ref[...] = h
    xhat = h * lax.rsqrt(jnp.mean(h * h, axis=-1, keepdims=True) + EPS)
    kv_ref[...] = _dot((xhat * gk_ref[...]).astype(BF16), wkv_ref[...]).astype(BF16)
    q_ref[...] = (_dot((xhat * gq_ref[...]).astype(BF16), wq_ref[...]) * q_scale).astype(BF16)


def _combine_proj(h, y0, y1, gt, g_kv, g_q, w_kv, w_q, q_scale):
    n = h.shape[0]
    tm = TM_PROJ
    row = lambda i: (i, 0)
    return pl.pallas_call(
        functools.partial(_combine_proj_kernel, q_scale=float(q_scale)),
        out_shape=(jax.ShapeDtypeStruct(h.shape, F32),
                   jax.ShapeDtypeStruct((n, w_kv.shape[1]), BF16),
                   jax.ShapeDtypeStruct((n, w_q.shape[1]), BF16)),
        grid=(n // tm,),
        in_specs=[pl.BlockSpec((tm, D_MODEL), row),
                  pl.BlockSpec((tm, D_MODEL), row),
                  pl.BlockSpec((tm, D_MODEL), row),
                  pl.BlockSpec((tm, TOP_K), row),
                  _resident(g_kv.shape),
                  _resident(g_q.shape),
                  _resident(w_kv.shape),
                  _resident(w_q.shape)],
        out_specs=(pl.BlockSpec((tm, D_MODEL), row),
                   pl.BlockSpec((tm, w_kv.shape[1]), row),
                   pl.BlockSpec((tm, w_q.shape[1]), row)),
        input_output_aliases={0: 0},
        compiler_params=_params("parallel"),
        name="combine_proj",
    )(h, y0, y1, gt, g_kv, g_q, w_kv, w_q)


def _combine_norm_kernel(h_ref, y0_ref, y1_ref, gt_ref, g_ref, o_ref):
    o_ref[...] = _rms(_mix(h_ref, y0_ref, y1_ref, gt_ref), g_ref[...])


def _combine_norm(h, y0, y1, gt, g):
    n = h.shape[0]
    tm = TM_PROJ
    row = lambda i: (i, 0)
    return pl.pallas_call(
        _combine_norm_kernel,
        out_shape=jax.ShapeDtypeStruct(h.shape, F32),
        grid=(n // tm,),
        in_specs=[pl.BlockSpec((tm, D_MODEL), row),
                  pl.BlockSpec((tm, D_MODEL), row),
                  pl.BlockSpec((tm, D_MODEL), row),
                  pl.BlockSpec((tm, TOP_K), row),
                  _resident(g.shape)],
        out_specs=pl.BlockSpec((tm, D_MODEL), row),
        input_output_aliases={0: 0},
        compiler_params=_params("parallel"),
        name="combine_norm",
    )(h, y0, y1, gt, g)


def _sb_attn_kernel(q_ref, k_ref, v_ref, u_ref, o_ref, *, tb, heads):
    tri = u_ref[...]
    nq = q_ref.shape[1] // tb
    below_diag = (lax.broadcasted_iota(jnp.int32, (tb, tb), 1)
                  < lax.broadcasted_iota(jnp.int32, (tb, tb), 0))

    def block(q, g, jb, r_sum, acc, diagonal):
        start = _aligned(jb * tb, tb)
        kb = k_ref[0, pl.ds(start, tb), g * SB_HEAD:(g + 1) * SB_HEAD]
        vb = v_ref[0, pl.ds(start, tb), g * SB_HEAD:(g + 1) * SB_HEAD]
        z = _dot_nt(q, kb)
        neg_abs = pltpu.bitcast(pltpu.bitcast(z, jnp.uint32) | jnp.uint32(0x80000000), F32)
        soft = jnp.log(1.0 + jnp.exp2(neg_abs)) * LOG2E
        ls = jnp.minimum(z, 0.0) - soft
        lb = ls - z
        if diagonal:
            lb = jnp.where(below_diag, lb, 0.0)
        between = _dot(lb.astype(BF16), tri)
        a = jnp.exp2(ls + between + r_sum)
        if diagonal:
            a = jnp.where(below_diag, a, 0.0)
        acc = acc + _dot(a.astype(BF16), vb)
        r_sum = r_sum + jnp.sum(lb, axis=-1, keepdims=True)
        return r_sum, acc

    def q_block(i, first):
        qs = _aligned(i * tb, tb)
        qg = [q_ref[0, pl.ds(qs, tb), g * SB_HEAD:(g + 1) * SB_HEAD] for g in range(heads)]
        zero_r, zero_acc = jnp.zeros((tb, 1), F32), jnp.zeros((tb, SB_HEAD), F32)
        state = [block(qg[g], g, i, zero_r, zero_acc, True) for g in range(heads)]
        if not first:
            state = [block(qg[g], g, i - 1, state[g][0], state[g][1], False) for g in range(heads)]

            def cond(c):
                live = jnp.max(c[1][0][0])
                for g in range(1, heads):
                    live = jnp.maximum(live, jnp.max(c[1][g][0]))
                return jnp.logical_and(c[0] >= 0, live > SB_EXIT_LOG2)

            def body(c):
                return c[0] - 1, [block(qg[g], g, c[0], c[1][g][0], c[1][g][1], False) for g in range(heads)]

            _, state = lax.while_loop(cond, body, (i - 2, state))
        for g in range(heads):
            o_ref[0, pl.ds(qs, tb), g * SB_HEAD:(g + 1) * SB_HEAD] = state[g][1].astype(BF16)

    q_block(0, True)

    def rest(i, carry):
        q_block(i, False)
        return carry

    lax.fori_loop(1, nq, rest, 0)


def _sb_attn(q, kv, tri):
    b, s, _ = q.shape
    hg = SB_HEADS_PER_STEP
    ng = SB_HEADS // hg
    w = hg * SB_HEAD
    return pl.pallas_call(
        functools.partial(_sb_attn_kernel, tb=TB_SB, heads=hg),
        out_shape=jax.ShapeDtypeStruct((b, s, SB_HEADS * SB_HEAD), BF16),
        grid=(b, ng),
        in_specs=[pl.BlockSpec((1, s, w), lambda bi, h: (bi, 0, h)),
                  pl.BlockSpec((1, s, w), lambda bi, h: (bi, 0, h)),
                  pl.BlockSpec((1, s, w), lambda bi, h: (bi, 0, ng + h)),
                  pl.BlockSpec((TB_SB, TB_SB), lambda bi, h: (0, 0))],
        out_specs=pl.BlockSpec((1, s, w), lambda bi, h: (bi, 0, h)),
        compiler_params=_params("parallel", "parallel"),
        name="sb_attn",
    )(q, kv, kv, tri)


def _rope_tables(positions):
    inv_freq = 1.0 / (ROPE_BASE ** (jnp.arange(0, QK_ROPE, 2, dtype=F32) / QK_ROPE))
    ang = positions.astype(F32).reshape(-1, 1) * inv_freq
    cos, sin = jnp.cos(ang), jnp.sin(ang)
    z32 = jnp.zeros_like(cos)
    z64 = jnp.zeros((cos.shape[0], LANES - QK_ROPE), F32)
    cos_t = jnp.concatenate([cos, cos, z64], axis=-1)
    sin_a = jnp.concatenate([-sin, z32, z64], axis=-1)
    sin_b = jnp.concatenate([z32, sin, z64], axis=-1)
    return cos_t, sin_a, sin_b


def _mla_weights(w_down, w_uq, w_ukv):
    wd = jnp.pad(w_down, ((0, 0), (0, DOWN_PAD - w_down.shape[1]))).astype(BF16)
    wuq = w_uq.reshape(Q_LORA, MLA_HEADS, QK_NOPE + QK_ROPE)
    wuq = jnp.pad(wuq, ((0, 0), (0, 0), (0, MLA_QK_PAD - QK_NOPE - QK_ROPE)))
    wuq = wuq.reshape(Q_LORA, MLA_HEADS * MLA_QK_PAD).astype(BF16)
    wukv = w_ukv.reshape(KV_LORA, MLA_HEADS, QK_NOPE + V_HEAD)
    wuk = wukv[:, :, :QK_NOPE].reshape(KV_LORA, MLA_HEADS * QK_NOPE).astype(BF16)
    wuv = wukv[:, :, QK_NOPE:].reshape(KV_LORA, MLA_HEADS * V_HEAD).astype(BF16)
    return wd, wuq, wuk, wuv


def kernel(x, positions, attn_norm_0, mla_w_down_0, mla_q_norm_0, mla_w_uq_0, mla_kv_norm_0, mla_w_ukv_0, mla_w_o_0, ffn_norm_0, ffn_w_gate_up_0, ffn_w_down_0, attn_norm_1, mla_w_down_1, mla_q_norm_1, mla_w_uq_1, mla_kv_norm_1, mla_w_ukv_1, mla_w_o_1, ffn_norm_1, moe_router_1, moe_w_gate_up_1, moe_w_down_1, kv_shared_norm, kv_shared_w, attn_norm_2, sb_w_q_2, sb_w_o_2, ffn_norm_2, ffn_w_gate_up_2, ffn_w_down_2, attn_norm_3, sb_w_q_3, sb_w_o_3, ffn_norm_3, moe_router_3, moe_w_gate_up_3, moe_w_down_3, final_norm):
    b, s, d = x.shape
    n = b * s
    row = lambda g: g.reshape(1, -1)
    h = x.reshape(n, d)
    cos_t, sin_a, sin_b = _rope_tables(positions)

    mla_layers = [
        (attn_norm_0, mla_w_down_0, mla_q_norm_0, mla_w_uq_0, mla_kv_norm_0, mla_w_ukv_0, mla_w_o_0),
        (attn_norm_1, mla_w_down_1, mla_q_norm_1, mla_w_uq_1, mla_kv_norm_1, mla_w_ukv_1, mla_w_o_1),
    ]
    tri = (lax.broadcasted_iota(jnp.int32, (TB_SB, TB_SB), 0)
           > lax.broadcasted_iota(jnp.int32, (TB_SB, TB_SB), 1)).astype(BF16)
    sb_scale = SB_HEAD ** -0.5 * LOG2E

    def ffn_weights(w_gate_up, w_down):
        return _to_bf16_halves(w_gate_up, 4) + _to_bf16_halves(w_down, 2)

    def mla_attention(h, layer, cast_gate_up, cast_down):
        an, w_down, qn, w_uq, kvn, w_ukv, _ = mla_layers[layer]
        wd, wuq, wuk, wuv = _mla_weights(w_down, w_uq, w_ukv)
        q, k, v = _mla_proj(h, row(an), wd, row(qn), row(kvn), wuq, wuk, wuv, cos_t, sin_a, sin_b)
        o, cast = _mla_attn(q.reshape(b, s, -1), k.reshape(b, s, -1), v.reshape(b, s, -1), cast_gate_up, cast_down)
        return o.reshape(n, -1), cast

    o, moe_w_1 = mla_attention(h, 0, moe_w_gate_up_1, moe_w_down_1)
    h = _dense_ffn(h, o, mla_w_o_0.astype(BF16), row(ffn_norm_0), ffn_weights(ffn_w_gate_up_0, ffn_w_down_0), False)
    o, moe_w_3 = mla_attention(h, 1, moe_w_gate_up_3, moe_w_down_3)
    h, y0, y1, gt = _moe_dispatch(h, o, mla_w_o_1.astype(BF16), row(ffn_norm_1), moe_router_1, moe_w_1)
    h, kv_shared, q = _combine_proj(h, y0, y1, gt, row(kv_shared_norm), row(attn_norm_2),
                                    kv_shared_w.astype(BF16), sb_w_q_2.astype(BF16), sb_scale)
    kv_shared = kv_shared.reshape(b, s, -1)
    o = _sb_attn(q.reshape(b, s, -1), kv_shared, tri).reshape(n, -1)
    h = _dense_ffn(h, o, sb_w_o_2.astype(BF16), row(ffn_norm_2), ffn_weights(ffn_w_gate_up_2, ffn_w_down_2), True)
    q = _norm_matmul(h, row(attn_norm_3), sb_w_q_3.astype(BF16), scale=sb_scale)
    o = _sb_attn(q.reshape(b, s, -1), kv_shared, tri).reshape(n, -1)
    h, y0, y1, gt = _moe_dispatch(h, o, sb_w_o_3.astype(BF16), row(ffn_norm_3), moe_router_3, moe_w_3)
    return _combine_norm(h, y0, y1, gt, row(final_norm)).reshape(b, s, d)
```

```python
import functools

import jax
import jax.numpy as jnp
from jax import lax
from jax.experimental import pallas as pl
from jax.experimental.pallas import tpu as pltpu

F32 = jnp.float32
BF16 = jnp.bfloat16

D_MODEL = 1024
CHUNK = 64
MLA_HEADS = 8
QK_NOPE = 128
QK_ROPE = 64
V_HEAD = 128
Q_LORA = 384
KV_LORA = 256
ROPE_BASE = 10000.0
SB_HEADS = 8
SB_HEAD = 128
D_FF = 2816
N_EXPERTS = 8
TOP_K = 2
EPS = 1e-6

LANES = 128
MLA_QK_PAD = 256
DOWN_PAD = 768
VMEM_LIMIT = 56 * 1024 * 1024

TM_PROJ = 512
TM_FFN = 512
TM_MOE = 256
MOE_CHUNKS = 4
CAST_BLOCK_BYTES = 12 * 1024 * 1024
TQ_MLA = 512
TK_MLA = 512
MLA_ROW_CHUNK = 32
MLA_HEADS_PER_STEP = 4
TB_SB = 256
SB_HEADS_PER_STEP = 4
TM_ROUTER = 512
LOG2E = 1.4426950408889634
SB_EXIT_LOG2 = -150.0


def _rms(x, g):
    ms = jnp.mean(x * x, axis=-1, keepdims=True)
    return x * lax.rsqrt(ms + EPS) * g


def _dot(a, b):
    return jnp.dot(a, b, preferred_element_type=F32)


def _dot_nt(a, b):
    return lax.dot_general(a, b, (((1,), (1,)), ((), ())), preferred_element_type=F32)


def _silu(g):
    return g / (1.0 + jnp.exp(-g))


def _aligned(x, m):
    return x if isinstance(x, int) else pl.multiple_of(x, m)


def _params(*sem):
    return pltpu.CompilerParams(dimension_semantics=sem, vmem_limit_bytes=VMEM_LIMIT)


def _mla_proj_kernel(h_ref, g_ref, wd_ref, qn_ref, kvn_ref, wuq_ref, wuk_ref, wuv_ref,
                     c_ref, sa_ref, sb_ref, q_ref, k_ref, v_ref, *, scale):
    xn = _rms(h_ref[...], g_ref[...]).astype(BF16)
    c = _dot(xn, wd_ref[...])
    cq = _rms(c[:, :Q_LORA], qn_ref[...]).astype(BF16)
    ckv = _rms(c[:, Q_LORA:Q_LORA + KV_LORA], kvn_ref[...]).astype(BF16)
    cos_t, sin_a, sin_b = c_ref[...], sa_ref[...], sb_ref[...]

    def rope(r):
        return r * cos_t + pltpu.roll(r, 96, 1) * sin_a + pltpu.roll(r, 32, 1) * sin_b

    kr = rope(c[:, Q_LORA + KV_LORA:]).astype(BF16)
    q = _dot(cq, wuq_ref[...])
    kn = _dot(ckv, wuk_ref[...])
    v_ref[...] = _dot(ckv, wuv_ref[...]).astype(BF16)
    for h in range(MLA_HEADS):
        lo = h * MLA_QK_PAD
        q_ref[:, lo:lo + LANES] = (q[:, lo:lo + LANES] * scale).astype(BF16)
        q_ref[:, lo + LANES:lo + 2 * LANES] = (rope(q[:, lo + LANES:lo + 2 * LANES]) * scale).astype(BF16)
        k_ref[:, lo:lo + LANES] = kn[:, h * LANES:(h + 1) * LANES].astype(BF16)
        k_ref[:, lo + LANES:lo + 2 * LANES] = kr


def _mla_proj(h, g, wd, qn, kvn, wuq, wuk, wuv, cos_t, sin_a, sin_b):
    n = h.shape[0]
    tm = TM_PROJ
    row = lambda i: (i, 0)
    fixed = lambda i: (0, 0)
    scale = float((QK_NOPE + QK_ROPE) ** -0.5 * LOG2E)
    return pl.pallas_call(
        functools.partial(_mla_proj_kernel, scale=scale),
        out_shape=(jax.ShapeDtypeStruct((n, MLA_HEADS * MLA_QK_PAD), BF16),
                   jax.ShapeDtypeStruct((n, MLA_HEADS * MLA_QK_PAD), BF16),
                   jax.ShapeDtypeStruct((n, MLA_HEADS * V_HEAD), BF16)),
        grid=(n // tm,),
        in_specs=[pl.BlockSpec((tm, D_MODEL), row),
                  pl.BlockSpec((1, D_MODEL), fixed),
                  pl.BlockSpec((D_MODEL, DOWN_PAD), fixed),
                  pl.BlockSpec((1, Q_LORA), fixed),
                  pl.BlockSpec((1, KV_LORA), fixed),
                  pl.BlockSpec((Q_LORA, MLA_HEADS * MLA_QK_PAD), fixed),
                  pl.BlockSpec((KV_LORA, MLA_HEADS * QK_NOPE), fixed),
                  pl.BlockSpec((KV_LORA, MLA_HEADS * V_HEAD), fixed),
                  pl.BlockSpec((tm, LANES), row),
                  pl.BlockSpec((tm, LANES), row),
                  pl.BlockSpec((tm, LANES), row)],
        out_specs=(pl.BlockSpec((tm, MLA_HEADS * MLA_QK_PAD), row),
                   pl.BlockSpec((tm, MLA_HEADS * MLA_QK_PAD), row),
                   pl.BlockSpec((tm, MLA_HEADS * V_HEAD), row)),
        compiler_params=_params("parallel"),
        name="mla_proj",
    )(h, g, wd, qn, kvn, wuq, wuk, wuv, cos_t, sin_a, sin_b)


def _mla_attn_kernel(q_ref, k_ref, v_ref, wgu_ref, wd_ref, o_ref, wg_ref, wu_ref, wda_ref, wdb_ref,
                     s_ref, p_ref, m_ref, l_ref, alpha_ref, acc_ref, *, tq, tk, heads):
    wg_ref[...] = wgu_ref[:, :D_FF].astype(BF16)
    wu_ref[...] = wgu_ref[:, D_FF:].astype(BF16)
    wda_ref[...] = wd_ref[:, :D_MODEL // 2].astype(BF16)
    wdb_ref[...] = wd_ref[:, D_MODEL // 2:].astype(BF16)

    i = pl.program_id(2)
    qg = [q_ref[0, :, g * MLA_QK_PAD:(g + 1) * MLA_QK_PAD] for g in range(heads)]
    rc = MLA_ROW_CHUNK

    m_ref[...] = jnp.full(m_ref.shape, -jnp.inf, F32)
    l_ref[...] = jnp.zeros(l_ref.shape, F32)
    acc_ref[...] = jnp.zeros(acc_ref.shape, F32)

    def scores(g, j):
        kb = k_ref[0, pl.ds(_aligned(j * tk, tk), tk), g * MLA_QK_PAD:(g + 1) * MLA_QK_PAD]
        s_ref[g] = _dot_nt(qg[g], kb)

    def update(g, j, diagonal):
        vb = v_ref[0, pl.ds(_aligned(j * tk, tk), tk), g * V_HEAD:(g + 1) * V_HEAD]
        for c in range(tq // rc):
            rows = slice(c * rc, (c + 1) * rc)
            s = s_ref[g, rows, :]
            if diagonal:
                qc = (lax.broadcasted_iota(jnp.int32, (rc, tk), 0) + c * rc) // CHUNK
                kc = lax.broadcasted_iota(jnp.int32, (rc, tk), 1) // CHUNK
                s = jnp.where(kc <= qc, s, -jnp.inf)
            m_old = m_ref[g, rows, :]
            m_new = jnp.maximum(m_old, jnp.max(s, axis=-1, keepdims=True))
            alpha = jnp.exp2(m_old - m_new)
            m_ref[g, rows, :] = m_new
            alpha_ref[g, rows, :] = alpha
            psum = jnp.zeros((rc, LANES), F32)
            for t in range(tk // LANES):
                p = jnp.exp2(s[:, t * LANES:(t + 1) * LANES] - m_new)
                psum = psum + p
                p_ref[g, rows, t * LANES:(t + 1) * LANES] = p.astype(BF16)
            l_ref[g, rows, :] = alpha * l_ref[g, rows, :] + jnp.sum(psum, axis=-1, keepdims=True)
        acc_ref[g] = alpha_ref[g] * acc_ref[g] + _dot(p_ref[g], vb)

    for g in range(heads):
        scores(g, 0)

    def body(j, carry):
        for g in range(heads):
            update(g, j, False)
            scores(g, j + 1)
        return carry

    lax.fori_loop(0, i, body, 0)
    for g in range(heads):
        update(g, i, True)
    for g in range(heads):
        o_ref[0, :, g * V_HEAD:(g + 1) * V_HEAD] = (acc_ref[g] / l_ref[g]).astype(BF16)


def _mla_attn(q, k, v, w_gate_up, w_down):
    b, s, _ = q.shape
    tq, tk, hg = TQ_MLA, TK_MLA, MLA_HEADS_PER_STEP
    ng, nq = MLA_HEADS // hg, s // tq
    steps = b * ng * nq
    wgu2 = w_gate_up.reshape(-1, 2 * D_FF)
    wd2 = w_down.reshape(-1, D_MODEL)
    r_gu, r_d = wgu2.shape[0] // steps, wd2.shape[0] // steps
    assert r_gu * steps == wgu2.shape[0] and r_d * steps == wd2.shape[0] and r_gu % 16 == 0 and r_d % 16 == 0
    slab = lambda bi, h, i: ((bi * ng + h) * nq + i, 0)
    o, wg, wu, wda, wdb = pl.pallas_call(
        functools.partial(_mla_attn_kernel, tq=tq, tk=tk, heads=hg),
        out_shape=(jax.ShapeDtypeStruct((b, s, MLA_HEADS * V_HEAD), BF16),
                   jax.ShapeDtypeStruct((wgu2.shape[0], D_FF), BF16),
                   jax.ShapeDtypeStruct((wgu2.shape[0], D_FF), BF16),
                   jax.ShapeDtypeStruct((wd2.shape[0], D_MODEL // 2), BF16),
                   jax.ShapeDtypeStruct((wd2.shape[0], D_MODEL // 2), BF16)),
        grid=(b, ng, nq),
        in_specs=[pl.BlockSpec((1, tq, hg * MLA_QK_PAD), lambda bi, h, i: (bi, i, h)),
                  pl.BlockSpec((1, s, hg * MLA_QK_PAD), lambda bi, h, i: (bi, 0, h)),
                  pl.BlockSpec((1, s, hg * V_HEAD), lambda bi, h, i: (bi, 0, h)),
                  pl.BlockSpec((r_gu, 2 * D_FF), slab),
                  pl.BlockSpec((r_d, D_MODEL), slab)],
        out_specs=(pl.BlockSpec((1, tq, hg * V_HEAD), lambda bi, h, i: (bi, i, h)),
                   pl.BlockSpec((r_gu, D_FF), slab),
                   pl.BlockSpec((r_gu, D_FF), slab),
                   pl.BlockSpec((r_d, D_MODEL // 2), slab),
                   pl.BlockSpec((r_d, D_MODEL // 2), slab)),
        scratch_shapes=[pltpu.VMEM((hg, tq, tk), F32), pltpu.VMEM((hg, tq, tk), BF16),
                        pltpu.VMEM((hg, tq, LANES), F32), pltpu.VMEM((hg, tq, LANES), F32),
                        pltpu.VMEM((hg, tq, LANES), F32), pltpu.VMEM((hg, tq, V_HEAD), F32)],
        compiler_params=_params("parallel", "parallel", "arbitrary"),
        name="mla_attn",
    )(q, k, v, wgu2, wd2)
    e = w_gate_up.shape[0]
    ffn_w = (wg.reshape(e, D_MODEL, D_FF), wu.reshape(e, D_MODEL, D_FF),
             wda.reshape(e, D_FF, D_MODEL // 2), wdb.reshape(e, D_FF, D_MODEL // 2))
    return o, ffn_w


def _cast_kernel(*refs, n_in, n_out):
    per = n_in // n_out
    for o in range(n_out):
        parts = [refs[o * per + k][...].astype(BF16) for k in range(per)]
        refs[n_in + o][...] = parts[0] if per == 1 else jnp.concatenate(parts, axis=1)


def _to_bf16_halves(w, n_in):
    n_out = 2
    cols = w.shape[-1]
    rows = w.size // cols
    wi, wo = cols // n_in, cols // n_out
    tr = max(r for r in range(16, rows + 1, 16) if rows % r == 0 and r * cols * 4 <= CAST_BLOCK_BYTES)
    outs = pl.pallas_call(
        functools.partial(_cast_kernel, n_in=n_in, n_out=n_out),
        out_shape=tuple(jax.ShapeDtypeStruct((rows, wo), BF16) for _ in range(n_out)),
        grid=(rows // tr,),
        in_specs=[pl.BlockSpec((tr, wi), functools.partial(lambda c, i: (i, c), c)) for c in range(n_in)],
        out_specs=tuple(pl.BlockSpec((tr, wo), lambda i: (i, 0)) for _ in range(n_out)),
        compiler_params=_params("parallel"),
        name="cast_bf16",
    )(*([w.reshape(rows, cols)] * n_in))
    return tuple(o.reshape(w.shape[:-1] + (wo,)) for o in outs)


def _norm_matmul_kernel(h_ref, g_ref, w_ref, o_ref, *, scale):
    xn = _rms(h_ref[...], g_ref[...]).astype(BF16)
    o_ref[...] = (_dot(xn, w_ref[...]) * scale).astype(BF16)


def _norm_matmul(h, g, w, scale=1.0):
    n = h.shape[0]
    tm = TM_PROJ
    f = w.shape[1]
    row = lambda i: (i, 0)
    return pl.pallas_call(
        functools.partial(_norm_matmul_kernel, scale=float(scale)),
        out_shape=jax.ShapeDtypeStruct((n, f), BF16),
        grid=(n // tm,),
        in_specs=[pl.BlockSpec((tm, D_MODEL), row),
                  pl.BlockSpec((1, D_MODEL), lambda i: (0, 0)),
                  pl.BlockSpec(w.shape, lambda i: (0, 0))],
        out_specs=pl.BlockSpec((tm, f), row),
        compiler_params=_params("parallel"),
        name="norm_matmul",
    )(h, g, w)


def _swiglu(x, w_refs):
    wg_ref, wu_ref, wda_ref, wdb_ref = w_refs
    gate = _dot(x, wg_ref[...])
    up = _dot(x, wu_ref[...])
    act = (_silu(gate) * up).astype(BF16)
    return jnp.concatenate([_dot(act, wda_ref[...]), _dot(act, wdb_ref[...])], axis=1)


def _ffn_kernel(h_ref, a_ref, wo_ref, g_ref, wg_ref, wu_ref, wda_ref, wdb_ref, o_ref):
    h = h_ref[...] + _dot(a_ref[...], wo_ref[...])
    x = _rms(h, g_ref[...]).astype(BF16)
    o_ref[...] = h + _swiglu(x, (wg_ref, wu_ref, wda_ref, wdb_ref))


def _resident(shape):
    return pl.BlockSpec(shape, lambda *_: (0,) * len(shape), pipeline_mode=pl.Buffered(1))


def _dense_ffn(h, a, w_o, g, ffn_w, in_place):
    n = h.shape[0]
    tm = TM_FFN
    row = lambda i: (i, 0)
    return pl.pallas_call(
        _ffn_kernel,
        out_shape=jax.ShapeDtypeStruct(h.shape, F32),
        grid=(n // tm,),
        in_specs=[pl.BlockSpec((tm, D_MODEL), row),
                  pl.BlockSpec((tm, D_MODEL), row),
                  _resident(w_o.shape),
                  _resident(g.shape)] + [_resident(w.shape) for w in ffn_w],
        out_specs=pl.BlockSpec((tm, D_MODEL), row),
        input_output_aliases={0: 0} if in_place else {},
        compiler_params=_params("parallel"),
        name="dense_ffn",
    )(h, a, w_o, g, *ffn_w)


def _router_kernel(h_ref, a_ref, wo_ref, g_ref, rhi_ref, rlo_ref, tri_ref,
                   h_out_ref, xn_ref, idx_ref, gate_ref, rank_ref, count_ref, base_ref):
    h = h_ref[...] + _dot(a_ref[...], wo_ref[...])
    h_out_ref[...] = h
    xn = _rms(h, g_ref[...])
    x_hi = xn.astype(BF16)
    x_lo = (xn - x_hi.astype(F32)).astype(BF16)
    xn_ref[...] = x_hi
    logits = _dot_nt(rhi_ref[...], x_hi) + _dot_nt(rhi_ref[...], x_lo) + _dot_nt(rlo_ref[...], x_hi)
    m = jnp.max(logits, axis=0, keepdims=True)
    e = jnp.exp(logits - m)
    p = e / jnp.sum(e, axis=0, keepdims=True)
    eid = lax.broadcasted_iota(jnp.int32, p.shape, 0)
    p1 = jnp.max(p, axis=0, keepdims=True)
    i1 = jnp.min(jnp.where(p == p1, eid, N_EXPERTS), axis=0, keepdims=True)
    pm = jnp.where(eid == i1, -1.0, p)
    p2 = jnp.max(pm, axis=0, keepdims=True)
    i2 = jnp.min(jnp.where(pm == p2, eid, N_EXPERTS), axis=0, keepdims=True)
    den = p1 + p2
    idx_ref[...] = jnp.concatenate([i1, i2], axis=0)
    gate_ref[...] = jnp.concatenate([p1 / den, p2 / den], axis=0)

    @pl.when(pl.program_id(0) == 0)
    def _():
        base_ref[...] = jnp.zeros(base_ref.shape, F32)

    first = (eid == i1).astype(F32)
    second = (eid == i2).astype(F32)
    earlier = tri_ref[...]
    before_first = _dot(first.astype(BF16), earlier)
    before_second = _dot(second.astype(BF16), earlier)
    n_first = jnp.sum(first, axis=1, keepdims=True)
    n_second = jnp.sum(second, axis=1, keepdims=True)
    base = base_ref[:, 0:1]
    rank_first = jnp.sum(first * (base + before_first), axis=0, keepdims=True)
    rank_second = jnp.sum(second * (base + n_first + before_second), axis=0, keepdims=True)
    rank_ref[...] = jnp.concatenate([rank_first, rank_second], axis=0).astype(jnp.int32)
    total = jnp.broadcast_to(base + n_first + n_second, base_ref.shape)
    base_ref[...] = total
    count_ref[...] = total.astype(jnp.int32)


def _router(h, a, w_o, g, r_hi, r_lo):
    n = h.shape[0]
    tm = TM_ROUTER
    row = lambda i: (i, 0)
    col = lambda i: (0, i)
    earlier = (lax.broadcasted_iota(jnp.int32, (tm, tm), 0)
               < lax.broadcasted_iota(jnp.int32, (tm, tm), 1)).astype(BF16)
    return pl.pallas_call(
        _router_kernel,
        out_shape=(jax.ShapeDtypeStruct((n, D_MODEL), F32),
                   jax.ShapeDtypeStruct((n, D_MODEL), BF16),
                   jax.ShapeDtypeStruct((TOP_K, n), jnp.int32),
                   jax.ShapeDtypeStruct((TOP_K, n), F32),
                   jax.ShapeDtypeStruct((TOP_K, n), jnp.int32),
                   jax.ShapeDtypeStruct((N_EXPERTS, LANES), jnp.int32)),
        grid=(n // tm,),
        in_specs=[pl.BlockSpec((tm, D_MODEL), row),
                  pl.BlockSpec((tm, D_MODEL), row),
                  _resident(w_o.shape),
                  _resident(g.shape),
                  _resident(r_hi.shape),
                  _resident(r_lo.shape),
                  _resident(earlier.shape)],
        out_specs=(pl.BlockSpec((tm, D_MODEL), row),
                   pl.BlockSpec((tm, D_MODEL), row),
                   pl.BlockSpec((TOP_K, tm), col),
                   pl.BlockSpec((TOP_K, tm), col),
                   pl.BlockSpec((TOP_K, tm), col),
                   pl.BlockSpec((N_EXPERTS, LANES), lambda i: (0, 0))),
        scratch_shapes=[pltpu.VMEM((N_EXPERTS, LANES), F32)],
        input_output_aliases={0: 0},
        compiler_params=_params("arbitrary"),
        name="moe_router",
    )(h, a, w_o, g, r_hi, r_lo, earlier)


def _moe_kernel(te_ref, nu_ref, x_ref, wg_ref, wu_ref, wda_ref, wdb_ref, *rest, tile0):
    o_ref = rest[-1]
    used = pl.program_id(0) + tile0 < nu_ref[0]

    @pl.when(used)
    def _():
        o_ref[...] = _swiglu(x_ref[...], (wg_ref.at[0], wu_ref.at[0], wda_ref.at[0], wdb_ref.at[0])).astype(BF16)

    @pl.when(jnp.logical_not(used))
    def _():
        o_ref[...] = jnp.zeros(o_ref.shape, BF16)


def _moe_experts(tile_expert, n_used, x_chunk, ffn_w, y_prev, tile0, p_rows):
    tm = TM_MOE
    in_specs = ([pl.BlockSpec((tm, D_MODEL), lambda t, te, nu: (t, 0))]
                + [pl.BlockSpec((1,) + w.shape[1:], lambda t, te, nu: (te[t + tile0], 0, 0)) for w in ffn_w])
    operands = [tile_expert, n_used, x_chunk, *ffn_w]
    aliases = {}
    if y_prev is not None:
        in_specs.append(pl.BlockSpec(memory_space=pl.ANY))
        operands.append(y_prev)
        aliases = {len(operands) - 1: 0}
    grid_spec = pltpu.PrefetchScalarGridSpec(
        num_scalar_prefetch=2,
        grid=(x_chunk.shape[0] // tm,),
        in_specs=in_specs,
        out_specs=pl.BlockSpec((tm, D_MODEL), lambda t, te, nu: (t + tile0, 0)),
    )
    return pl.pallas_call(
        functools.partial(_moe_kernel, tile0=tile0),
        out_shape=jax.ShapeDtypeStruct((p_rows, D_MODEL), BF16),
        grid_spec=grid_spec,
        input_output_aliases=aliases,
        compiler_params=_params("arbitrary"),
        name="moe_experts",
    )(*operands)


def _moe_dispatch(h, a, w_o, g, router, ffn_w):
    n = h.shape[0]
    tm, tr = TM_MOE, TM_ROUTER
    r_t = router.T
    r_hi = r_t.astype(BF16)
    r_lo = (r_t - r_hi.astype(F32)).astype(BF16)
    h, xn, idx, gates, rank, count = _router(h, a, w_o, g, r_hi, r_lo)

    experts = jnp.arange(N_EXPERTS, dtype=jnp.int32)
    counts = count[:, 0]
    padded = ((counts + tm - 1) // tm) * tm
    ends = jnp.cumsum(padded)
    offsets = ends - padded
    starts = jnp.cumsum(counts) - counts
    pos = rank + jnp.sum(jnp.where(idx[None] == experts[:, None, None], offsets[:, None, None], 0), axis=0)
    p_rows = TOP_K * n + N_EXPERTS * tm
    n_tiles = p_rows // tm
    n_used = (ends[-1] // tm).astype(jnp.int32)
    tile_start = jnp.arange(n_tiles, dtype=jnp.int32) * tm
    tile_expert = jnp.sum((tile_start[:, None] >= ends[None, :]).astype(jnp.int32), axis=1)
    tile_expert = jnp.minimum(tile_expert, N_EXPERTS - 1)
    last_expert = tile_expert[jnp.maximum(n_used - 1, 0)]
    tile_expert = jnp.where(jnp.arange(n_tiles) < n_used, tile_expert, last_expert)
    e_flat = idx.reshape(TOP_K, n // tr, tr).transpose(1, 0, 2).reshape(-1)
    order = jnp.argsort(e_flat, stable=True).astype(jnp.int32)
    order_tok = (order // (TOP_K * tr)) * tr + order % tr
    within = jnp.arange(p_rows, dtype=jnp.int32) - jnp.repeat(offsets[tile_expert], tm)
    src = jnp.repeat(starts[tile_expert], tm) + within
    valid = within < jnp.repeat(counts[tile_expert], tm)
    tok_sorted = jnp.where(valid, order_tok[jnp.clip(src, 0, TOP_K * n - 1)], 0)
    y_sorted = None
    chunk_tiles = n_tiles // MOE_CHUNKS
    for c in range(MOE_CHUNKS):
        rows = slice(c * chunk_tiles * tm, (c + 1) * chunk_tiles * tm)
        x_chunk = jnp.take(xn, tok_sorted[rows], axis=0, mode="clip")
        y_sorted = _moe_experts(tile_expert, n_used.reshape(1), x_chunk, ffn_w, y_sorted, c * chunk_tiles, p_rows)
    y0 = jnp.take(y_sorted, pos[0], axis=0, mode="clip")
    y1 = jnp.take(y_sorted, pos[1], axis=0, mode="clip")
    return h, y0, y1, gates.T


def _mix(h_ref, y0_ref, y1_ref, gt_ref):
    gt = gt_ref[...]
    return h_ref[...] + gt[:, 0:1] * y0_ref[...].astype(F32) + gt[:, 1:2] * y1_ref[...].astype(F32)


def _combine_proj_kernel(h_ref, y0_ref, y1_ref, gt_ref, gk_ref, gq_ref, wkv_ref, wq_ref,
                         h_out_ref, kv_ref, q_ref, *, q_scale):
    h = _mix(h_ref, y0_ref, y1_ref, gt_ref)
    h_out_ref[...] = h
    xhat = h * lax.rsqrt(jnp.mean(h * h, axis=-1, keepdims=True) + EPS)
    kv_ref[...] = _dot((xhat * gk_ref[...]).astype(BF16), wkv_ref[...]).astype(BF16)
    q_ref[...] = (_dot((xhat * gq_ref[...]).astype(BF16), wq_ref[...]) * q_scale).astype(BF16)


def _combine_proj(h, y0, y1, gt, g_kv, g_q, w_kv, w_q, q_scale):
    n = h.shape[0]
    tm = TM_PROJ
    row = lambda i: (i, 0)
    return pl.pallas_call(
        functools.partial(_combine_proj_kernel, q_scale=float(q_scale)),
        out_shape=(jax.ShapeDtypeStruct(h.shape, F32),
                   jax.ShapeDtypeStruct((n, w_kv.shape[1]), BF16),
                   jax.ShapeDtypeStruct((n, w_q.shape[1]), BF16)),
        grid=(n // tm,),
        in_specs=[pl.BlockSpec((tm, D_MODEL), row),
                  pl.BlockSpec((tm, D_MODEL), row),
                  pl.BlockSpec((tm, D_MODEL), row),
                  pl.BlockSpec((tm, TOP_K), row),
                  _resident(g_kv.shape),
                  _resident(g_q.shape),
                  _resident(w_kv.shape),
                  _resident(w_q.shape)],
        out_specs=(pl.BlockSpec((tm, D_MODEL), row),
                   pl.BlockSpec((tm, w_kv.shape[1]), row),
                   pl.BlockSpec((tm, w_q.shape[1]), row)),
        input_output_aliases={0: 0},
        compiler_params=_params("parallel"),
        name="combine_proj",
    )(h, y0, y1, gt, g_kv, g_q, w_kv, w_q)


def _combine_norm_kernel(h_ref, y0_ref, y1_ref, gt_ref, g_ref, o_ref):
    o_ref[...] = _rms(_mix(h_ref, y0_ref, y1_ref, gt_ref), g_ref[...])


def _combine_norm(h, y0, y1, gt, g):
    n = h.shape[0]
    tm = TM_PROJ
    row = lambda i: (i, 0)
    return pl.pallas_call(
        _combine_norm_kernel,
        out_shape=jax.ShapeDtypeStruct(h.shape, F32),
        grid=(n // tm,),
        in_specs=[pl.BlockSpec((tm, D_MODEL), row),
                  pl.BlockSpec((tm, D_MODEL), row),
                  pl.BlockSpec((tm, D_MODEL), row),
                  pl.BlockSpec((tm, TOP_K), row),
                  _resident(g.shape)],
        out_specs=pl.BlockSpec((tm, D_MODEL), row),
        input_output_aliases={0: 0},
        compiler_params=_params("parallel"),
        name="combine_norm",
    )(h, y0, y1, gt, g)


def _sb_attn_kernel(q_ref, k_ref, v_ref, u_ref, o_ref, *, tb, heads):
    tri = u_ref[...]
    nq = q_ref.shape[1] // tb
    below_diag = (lax.broadcasted_iota(jnp.int32, (tb, tb), 1)
                  < lax.broadcasted_iota(jnp.int32, (tb, tb), 0))

    def block(q, g, jb, r_sum, acc, diagonal):
        start = _aligned(jb * tb, tb)
        kb = k_ref[0, pl.ds(start, tb), g * SB_HEAD:(g + 1) * SB_HEAD]
        vb = v_ref[0, pl.ds(start, tb), g * SB_HEAD:(g + 1) * SB_HEAD]
        z = _dot_nt(q, kb)
        neg_abs = pltpu.bitcast(pltpu.bitcast(z, jnp.uint32) | jnp.uint32(0x80000000), F32)
        soft = jnp.log(1.0 + jnp.exp2(neg_abs)) * LOG2E
        ls = jnp.minimum(z, 0.0) - soft
        lb = ls - z
        if diagonal:
            lb = jnp.where(below_diag, lb, 0.0)
        between = _dot(lb.astype(BF16), tri)
        a = jnp.exp2(ls + between + r_sum)
        if diagonal:
            a = jnp.where(below_diag, a, 0.0)
        acc = acc + _dot(a.astype(BF16), vb)
        r_sum = r_sum + jnp.sum(lb, axis=-1, keepdims=True)
        return r_sum, acc

    def q_block(i, first):
        qs = _aligned(i * tb, tb)
        qg = [q_ref[0, pl.ds(qs, tb), g * SB_HEAD:(g + 1) * SB_HEAD] for g in range(heads)]
        zero_r, zero_acc = jnp.zeros((tb, 1), F32), jnp.zeros((tb, SB_HEAD), F32)
        state = [block(qg[g], g, i, zero_r, zero_acc, True) for g in range(heads)]
        if not first:
            state = [block(qg[g], g, i - 1, state[g][0], state[g][1], False) for g in range(heads)]

            def cond(c):
                live = jnp.max(c[1][0][0])
                for g in range(1, heads):
                    live = jnp.maximum(live, jnp.max(c[1][g][0]))
                return jnp.logical_and(c[0] >= 0, live > SB_EXIT_LOG2)

            def body(c):
                return c[0] - 1, [block(qg[g], g, c[0], c[1][g][0], c[1][g][1], False) for g in range(heads)]

            _, state = lax.while_loop(cond, body, (i - 2, state))
        for g in range(heads):
            o_ref[0, pl.ds(qs, tb), g * SB_HEAD:(g + 1) * SB_HEAD] = state[g][1].astype(BF16)

    q_block(0, True)

    def rest(i, carry):
        q_block(i, False)
        return carry

    lax.fori_loop(1, nq, rest, 0)


def _sb_attn(q, kv, tri):
    b, s, _ = q.shape
    hg = SB_HEADS_PER_STEP
    ng = SB_HEADS // hg
    w = hg * SB_HEAD
    return pl.pallas_call(
        functools.partial(_sb_attn_kernel, tb=TB_SB, heads=hg),
        out_shape=jax.ShapeDtypeStruct((b, s, SB_HEADS * SB_HEAD), BF16),
        grid=(b, ng),
        in_specs=[pl.BlockSpec((1, s, w), lambda bi, h: (bi, 0, h)),
                  pl.BlockSpec((1, s, w), lambda bi, h: (bi, 0, h)),
                  pl.BlockSpec((1, s, w), lambda bi, h: (bi, 0, ng + h)),
                  pl.BlockSpec((TB_SB, TB_SB), lambda bi, h: (0, 0))],
        out_specs=pl.BlockSpec((1, s, w), lambda bi, h: (bi, 0, h)),
        compiler_params=_params("parallel", "parallel"),
        name="sb_attn",
    )(q, kv, kv, tri)


def _rope_tables(positions):
    inv_freq = 1.0 / (ROPE_BASE ** (jnp.arange(0, QK_ROPE, 2, dtype=F32) / QK_ROPE))
    ang = positions.astype(F32).reshape(-1, 1) * inv_freq
    cos, sin = jnp.cos(ang), jnp.sin(ang)
    z32 = jnp.zeros_like(cos)
    z64 = jnp.zeros((cos.shape[0], LANES - QK_ROPE), F32)
    cos_t = jnp.concatenate([cos, cos, z64], axis=-1)
    sin_a = jnp.concatenate([-sin, z32, z64], axis=-1)
    sin_b = jnp.concatenate([z32, sin, z64], axis=-1)
    return cos_t, sin_a, sin_b


def _mla_weights(w_down, w_uq, w_ukv):
    wd = jnp.pad(w_down, ((0, 0), (0, DOWN_PAD - w_down.shape[1]))).astype(BF16)
    wuq = w_uq.reshape(Q_LORA, MLA_HEADS, QK_NOPE + QK_ROPE)
    wuq = jnp.pad(wuq, ((0, 0), (0, 0), (0, MLA_QK_PAD - QK_NOPE - QK_ROPE)))
    wuq = wuq.reshape(Q_LORA, MLA_HEADS * MLA_QK_PAD).astype(BF16)
    wukv = w_ukv.reshape(KV_LORA, MLA_HEADS, QK_NOPE + V_HEAD)
    wuk = wukv[:, :, :QK_NOPE].reshape(KV_LORA, MLA_HEADS * QK_NOPE).astype(BF16)
    wuv = wukv[:, :, QK_NOPE:].reshape(KV_LORA, MLA_HEADS * V_HEAD).astype(BF16)
    return wd, wuq, wuk, wuv


def kernel(x, positions, attn_norm_0, mla_w_down_0, mla_q_norm_0, mla_w_uq_0, mla_kv_norm_0, mla_w_ukv_0, mla_w_o_0, ffn_norm_0, ffn_w_gate_up_0, ffn_w_down_0, attn_norm_1, mla_w_down_1, mla_q_norm_1, mla_w_uq_1, mla_kv_norm_1, mla_w_ukv_1, mla_w_o_1, ffn_norm_1, moe_router_1, moe_w_gate_up_1, moe_w_down_1, kv_shared_norm, kv_shared_w, attn_norm_2, sb_w_q_2, sb_w_o_2, ffn_norm_2, ffn_w_gate_up_2, ffn_w_down_2, attn_norm_3, sb_w_q_3, sb_w_o_3, ffn_norm_3, moe_router_3, moe_w_gate_up_3, moe_w_down_3, final_norm):
    b, s, d = x.shape
    n = b * s
    row = lambda g: g.reshape(1, -1)
    h = x.reshape(n, d)
    cos_t, sin_a, sin_b = _rope_tables(positions)

    mla_layers = [
        (attn_norm_0, mla_w_down_0, mla_q_norm_0, mla_w_uq_0, mla_kv_norm_0, mla_w_ukv_0, mla_w_o_0),
        (attn_norm_1, mla_w_down_1, mla_q_norm_1, mla_w_uq_1, mla_kv_norm_1, mla_w_ukv_1, mla_w_o_1),
    ]
    tri = (lax.broadcasted_iota(jnp.int32, (TB_SB, TB_SB), 0)
           > lax.broadcasted_iota(jnp.int32, (TB_SB, TB_SB), 1)).astype(BF16)
    sb_scale = SB_HEAD ** -0.5 * LOG2E

    def ffn_weights(w_gate_up, w_down):
        return _to_bf16_halves(w_gate_up, 4) + _to_bf16_halves(w_down, 2)

    def mla_attention(h, layer, cast_gate_up, cast_down):
        an, w_down, qn, w_uq, kvn, w_ukv, _ = mla_layers[layer]
        wd, wuq, wuk, wuv = _mla_weights(w_down, w_uq, w_ukv)
        q, k, v = _mla_proj(h, row(an), wd, row(qn), row(kvn), wuq, wuk, wuv, cos_t, sin_a, sin_b)
        o, cast = _mla_attn(q.reshape(b, s, -1), k.reshape(b, s, -1), v.reshape(b, s, -1), cast_gate_up, cast_down)
        return o.reshape(n, -1), cast

    o, moe_w_1 = mla_attention(h, 0, moe_w_gate_up_1, moe_w_down_1)
    h = _dense_ffn(h, o, mla_w_o_0.astype(BF16), row(ffn_norm_0), ffn_weights(ffn_w_gate_up_0, ffn_w_down_0), False)
    o, moe_w_3 = mla_attention(h, 1, moe_w_gate_up_3, moe_w_down_3)
    h, y0, y1, gt = _moe_dispatch(h, o, mla_w_o_1.astype(BF16), row(ffn_norm_1), moe_router_1, moe_w_1)
    h, kv_shared, q = _combine_proj(h, y0, y1, gt, row(kv_shared_norm), row(attn_norm_2),
                                    kv_shared_w.astype(BF16), sb_w_q_2.astype(BF16), sb_scale)
    kv_shared = kv_shared.reshape(b, s, -1)
    o = _sb_attn(q.reshape(b, s, -1), kv_shared, tri).reshape(n, -1)
    h = _dense_ffn(h, o, sb_w_o_2.astype(BF16), row(ffn_norm_2), ffn_weights(ffn_w_gate_up_2, ffn_w_down_2), True)
    q = _norm_matmul(h, row(attn_norm_3), sb_w_q_3.astype(BF16), scale=sb_scale)
    o = _sb_attn(q.reshape(b, s, -1), kv_shared, tri).reshape(n, -1)
    h, y0, y1, gt = _moe_dispatch(h, o, sb_w_o_3.astype(BF16), row(ffn_norm_3), moe_router_3, moe_w_3)
    return _combine_norm(h, y0, y1, gt, row(final_norm)).reshape(b, s, d)
```

```python
import functools

import jax
import jax.numpy as jnp
from jax import lax
from jax.experimental import pallas as pl
from jax.experimental.pallas import tpu as pltpu

F32 = jnp.float32
BF16 = jnp.bfloat16

D_MODEL = 1024
CHUNK = 64
MLA_HEADS = 8
QK_NOPE = 128
QK_ROPE = 64
V_HEAD = 128
Q_LORA = 384
KV_LORA = 256
ROPE_BASE = 10000.0
SB_HEADS = 8
SB_HEAD = 128
D_FF = 2816
N_EXPERTS = 8
TOP_K = 2
EPS = 1e-6

LANES = 128
MLA_QK_PAD = 256
DOWN_PAD = 768
VMEM_LIMIT = 56 * 1024 * 1024

TM_PROJ = 512
TM_FFN = 512
TM_MOE = 256
MOE_CHUNKS = 4
COMBINE_CHUNKS = 4
CAST_BLOCK_BYTES = 12 * 1024 * 1024
TQ_MLA = 512
TK_MLA = 512
MLA_ROW_CHUNK = 32
MLA_HEADS_PER_STEP = 4
TB_SB = 256
SB_HEADS_PER_STEP = 4
TM_ROUTER = 512
LOG2E = 1.4426950408889634
SB_EXIT_LOG2 = -150.0


def _rms(x, g):
    ms = jnp.mean(x * x, axis=-1, keepdims=True)
    return x * lax.rsqrt(ms + EPS) * g


def _dot(a, b):
    return jnp.dot(a, b, preferred_element_type=F32)


def _dot_nt(a, b):
    return lax.dot_general(a, b, (((1,), (1,)), ((), ())), preferred_element_type=F32)


def _silu(g):
    return g / (1.0 + jnp.exp(-g))


def _aligned(x, m):
    return x if isinstance(x, int) else pl.multiple_of(x, m)


def _params(*sem):
    return pltpu.CompilerParams(dimension_semantics=sem, vmem_limit_bytes=VMEM_LIMIT)


def _mla_proj_kernel(h_ref, g_ref, wd_ref, qn_ref, kvn_ref, wuq_ref, wuk_ref, wuv_ref,
                     c_ref, sa_ref, sb_ref, q_ref, k_ref, v_ref, *, scale):
    xn = _rms(h_ref[...], g_ref[...]).astype(BF16)
    c = _dot(xn, wd_ref[...])
    cq = _rms(c[:, :Q_LORA], qn_ref[...]).astype(BF16)
    ckv = _rms(c[:, Q_LORA:Q_LORA + KV_LORA], kvn_ref[...]).astype(BF16)
    cos_t, sin_a, sin_b = c_ref[...], sa_ref[...], sb_ref[...]

    def rope(r):
        return r * cos_t + pltpu.roll(r, 96, 1) * sin_a + pltpu.roll(r, 32, 1) * sin_b

    kr = rope(c[:, Q_LORA + KV_LORA:]).astype(BF16)
    q = _dot(cq, wuq_ref[...])
    kn = _dot(ckv, wuk_ref[...])
    v_ref[...] = _dot(ckv, wuv_ref[...]).astype(BF16)
    for h in range(MLA_HEADS):
        lo = h * MLA_QK_PAD
        q_ref[:, lo:lo + LANES] = (q[:, lo:lo + LANES] * scale).astype(BF16)
        q_ref[:, lo + LANES:lo + 2 * LANES] = (rope(q[:, lo + LANES:lo + 2 * LANES]) * scale).astype(BF16)
        k_ref[:, lo:lo + LANES] = kn[:, h * LANES:(h + 1) * LANES].astype(BF16)
        k_ref[:, lo + LANES:lo + 2 * LANES] = kr


def _mla_proj(h, g, wd, qn, kvn, wuq, wuk, wuv, cos_t, sin_a, sin_b):
    n = h.shape[0]
    tm = TM_PROJ
    row = lambda i: (i, 0)
    fixed = lambda i: (0, 0)
    scale = float((QK_NOPE + QK_ROPE) ** -0.5 * LOG2E)
    return pl.pallas_call(
        functools.partial(_mla_proj_kernel, scale=scale),
        out_shape=(jax.ShapeDtypeStruct((n, MLA_HEADS * MLA_QK_PAD), BF16),
                   jax.ShapeDtypeStruct((n, MLA_HEADS * MLA_QK_PAD), BF16),
                   jax.ShapeDtypeStruct((n, MLA_HEADS * V_HEAD), BF16)),
        grid=(n // tm,),
        in_specs=[pl.BlockSpec((tm, D_MODEL), row),
                  pl.BlockSpec((1, D_MODEL), fixed),
                  pl.BlockSpec((D_MODEL, DOWN_PAD), fixed),
                  pl.BlockSpec((1, Q_LORA), fixed),
                  pl.BlockSpec((1, KV_LORA), fixed),
                  pl.BlockSpec((Q_LORA, MLA_HEADS * MLA_QK_PAD), fixed),
                  pl.BlockSpec((KV_LORA, MLA_HEADS * QK_NOPE), fixed),
                  pl.BlockSpec((KV_LORA, MLA_HEADS * V_HEAD), fixed),
                  pl.BlockSpec((tm, LANES), row),
                  pl.BlockSpec((tm, LANES), row),
                  pl.BlockSpec((tm, LANES), row)],
        out_specs=(pl.BlockSpec((tm, MLA_HEADS * MLA_QK_PAD), row),
                   pl.BlockSpec((tm, MLA_HEADS * MLA_QK_PAD), row),
                   pl.BlockSpec((tm, MLA_HEADS * V_HEAD), row)),
        compiler_params=_params("parallel"),
        name="mla_proj",
    )(h, g, wd, qn, kvn, wuq, wuk, wuv, cos_t, sin_a, sin_b)


def _mla_attn_kernel(q_ref, k_ref, v_ref, wgu_ref, wd_ref, o_ref, wg_ref, wu_ref, wda_ref, wdb_ref,
                     s_ref, p_ref, m_ref, l_ref, alpha_ref, acc_ref, *, tq, tk, heads):
    wg_ref[...] = wgu_ref[:, :D_FF].astype(BF16)
    wu_ref[...] = wgu_ref[:, D_FF:].astype(BF16)
    wda_ref[...] = wd_ref[:, :D_MODEL // 2].astype(BF16)
    wdb_ref[...] = wd_ref[:, D_MODEL // 2:].astype(BF16)

    i = pl.program_id(2)
    qg = [q_ref[0, :, g * MLA_QK_PAD:(g + 1) * MLA_QK_PAD] for g in range(heads)]
    rc = MLA_ROW_CHUNK

    m_ref[...] = jnp.full(m_ref.shape, -jnp.inf, F32)
    l_ref[...] = jnp.zeros(l_ref.shape, F32)
    acc_ref[...] = jnp.zeros(acc_ref.shape, F32)

    def scores(g, j):
        kb = k_ref[0, pl.ds(_aligned(j * tk, tk), tk), g * MLA_QK_PAD:(g + 1) * MLA_QK_PAD]
        s_ref[g] = _dot_nt(qg[g], kb)

    def update(g, j, diagonal):
        vb = v_ref[0, pl.ds(_aligned(j * tk, tk), tk), g * V_HEAD:(g + 1) * V_HEAD]
        for c in range(tq // rc):
            rows = slice(c * rc, (c + 1) * rc)
            s = s_ref[g, rows, :]
            if diagonal:
                qc = (lax.broadcasted_iota(jnp.int32, (rc, tk), 0) + c * rc) // CHUNK
                kc = lax.broadcasted_iota(jnp.int32, (rc, tk), 1) // CHUNK
                s = jnp.where(kc <= qc, s, -jnp.inf)
            m_old = m_ref[g, rows, :]
            m_new = jnp.maximum(m_old, jnp.max(s, axis=-1, keepdims=True))
            alpha = jnp.exp2(m_old - m_new)
            m_ref[g, rows, :] = m_new
            alpha_ref[g, rows, :] = alpha
            psum = jnp.zeros((rc, LANES), F32)
            for t in range(tk // LANES):
                p = jnp.exp2(s[:, t * LANES:(t + 1) * LANES] - m_new)
                psum = psum + p
                p_ref[g, rows, t * LANES:(t + 1) * LANES] = p.astype(BF16)
            l_ref[g, rows, :] = alpha * l_ref[g, rows, :] + jnp.sum(psum, axis=-1, keepdims=True)
        acc_ref[g] = alpha_ref[g] * acc_ref[g] + _dot(p_ref[g], vb)

    for g in range(heads):
        scores(g, 0)

    def body(j, carry):
        for g in range(heads):
            update(g, j, False)
            scores(g, j + 1)
        return carry

    lax.fori_loop(0, i, body, 0)
    for g in range(heads):
        update(g, i, True)
    for g in range(heads):
        o_ref[0, :, g * V_HEAD:(g + 1) * V_HEAD] = (acc_ref[g] / l_ref[g]).astype(BF16)


def _mla_attn(q, k, v, w_gate_up, w_down):
    b, s, _ = q.shape
    tq, tk, hg = TQ_MLA, TK_MLA, MLA_HEADS_PER_STEP
    ng, nq = MLA_HEADS // hg, s // tq
    steps = b * ng * nq
    wgu2 = w_gate_up.reshape(-1, 2 * D_FF)
    wd2 = w_down.reshape(-1, D_MODEL)
    r_gu, r_d = wgu2.shape[0] // steps, wd2.shape[0] // steps
    assert r_gu * steps == wgu2.shape[0] and r_d * steps == wd2.shape[0] and r_gu % 16 == 0 and r_d % 16 == 0
    slab = lambda bi, h, i: ((bi * ng + h) * nq + i, 0)
    o, wg, wu, wda, wdb = pl.pallas_call(
        functools.partial(_mla_attn_kernel, tq=tq, tk=tk, heads=hg),
        out_shape=(jax.ShapeDtypeStruct((b, s, MLA_HEADS * V_HEAD), BF16),
                   jax.ShapeDtypeStruct((wgu2.shape[0], D_FF), BF16),
                   jax.ShapeDtypeStruct((wgu2.shape[0], D_FF), BF16),
                   jax.ShapeDtypeStruct((wd2.shape[0], D_MODEL // 2), BF16),
                   jax.ShapeDtypeStruct((wd2.shape[0], D_MODEL // 2), BF16)),
        grid=(b, ng, nq),
        in_specs=[pl.BlockSpec((1, tq, hg * MLA_QK_PAD), lambda bi, h, i: (bi, i, h)),
                  pl.BlockSpec((1, s, hg * MLA_QK_PAD), lambda bi, h, i: (bi, 0, h)),
                  pl.BlockSpec((1, s, hg * V_HEAD), lambda bi, h, i: (bi, 0, h)),
                  pl.BlockSpec((r_gu, 2 * D_FF), slab),
                  pl.BlockSpec((r_d, D_MODEL), slab)],
        out_specs=(pl.BlockSpec((1, tq, hg * V_HEAD), lambda bi, h, i: (bi, i, h)),
                   pl.BlockSpec((r_gu, D_FF), slab),
                   pl.BlockSpec((r_gu, D_FF), slab),
                   pl.BlockSpec((r_d, D_MODEL // 2), slab),
                   pl.BlockSpec((r_d, D_MODEL // 2), slab)),
        scratch_shapes=[pltpu.VMEM((hg, tq, tk), F32), pltpu.VMEM((hg, tq, tk), BF16),
                        pltpu.VMEM((hg, tq, LANES), F32), pltpu.VMEM((hg, tq, LANES), F32),
                        pltpu.VMEM((hg, tq, LANES), F32), pltpu.VMEM((hg, tq, V_HEAD), F32)],
        compiler_params=_params("parallel", "parallel", "arbitrary"),
        name="mla_attn",
    )(q, k, v, wgu2, wd2)
    e = w_gate_up.shape[0]
    ffn_w = (wg.reshape(e, D_MODEL, D_FF), wu.reshape(e, D_MODEL, D_FF),
             wda.reshape(e, D_FF, D_MODEL // 2), wdb.reshape(e, D_FF, D_MODEL // 2))
    return o, ffn_w


def _cast_kernel(*refs, n_in, n_out):
    per = n_in // n_out
    for o in range(n_out):
        parts = [refs[o * per + k][...].astype(BF16) for k in range(per)]
        refs[n_in + o][...] = parts[0] if per == 1 else jnp.concatenate(parts, axis=1)


def _to_bf16_halves(w, n_in):
    n_out = 2
    cols = w.shape[-1]
    rows = w.size // cols
    wi, wo = cols // n_in, cols // n_out
    tr = max(r for r in range(16, rows + 1, 16) if rows % r == 0 and r * cols * 4 <= CAST_BLOCK_BYTES)
    outs = pl.pallas_call(
        functools.partial(_cast_kernel, n_in=n_in, n_out=n_out),
        out_shape=tuple(jax.ShapeDtypeStruct((rows, wo), BF16) for _ in range(n_out)),
        grid=(rows // tr,),
        in_specs=[pl.BlockSpec((tr, wi), functools.partial(lambda c, i: (i, c), c)) for c in range(n_in)],
        out_specs=tuple(pl.BlockSpec((tr, wo), lambda i: (i, 0)) for _ in range(n_out)),
        compiler_params=_params("parallel"),
        name="cast_bf16",
    )(*([w.reshape(rows, cols)] * n_in))
    return tuple(o.reshape(w.shape[:-1] + (wo,)) for o in outs)


def _norm_matmul_kernel(h_ref, g_ref, w_ref, o_ref, *, scale):
    xn = _rms(h_ref[...], g_ref[...]).astype(BF16)
    o_ref[...] = (_dot(xn, w_ref[...]) * scale).astype(BF16)


def _norm_matmul(h, g, w, scale=1.0):
    n = h.shape[0]
    tm = TM_PROJ
    f = w.shape[1]
    row = lambda i: (i, 0)
    return pl.pallas_call(
        functools.partial(_norm_matmul_kernel, scale=float(scale)),
        out_shape=jax.ShapeDtypeStruct((n, f), BF16),
        grid=(n // tm,),
        in_specs=[pl.BlockSpec((tm, D_MODEL), row),
                  pl.BlockSpec((1, D_MODEL), lambda i: (0, 0)),
                  pl.BlockSpec(w.shape, lambda i: (0, 0))],
        out_specs=pl.BlockSpec((tm, f), row),
        compiler_params=_params("parallel"),
        name="norm_matmul",
    )(h, g, w)


def _swiglu(x, w_refs):
    wg_ref, wu_ref, wda_ref, wdb_ref = w_refs
    gate = _dot(x, wg_ref[...])
    up = _dot(x, wu_ref[...])
    act = (_silu(gate) * up).astype(BF16)
    return jnp.concatenate([_dot(act, wda_ref[...]), _dot(act, wdb_ref[...])], axis=1)


def _ffn_kernel(h_ref, a_ref, wo_ref, g_ref, wg_ref, wu_ref, wda_ref, wdb_ref, o_ref):
    h = h_ref[...] + _dot(a_ref[...], wo_ref[...])
    x = _rms(h, g_ref[...]).astype(BF16)
    o_ref[...] = h + _swiglu(x, (wg_ref, wu_ref, wda_ref, wdb_ref))


def _resident(shape):
    return pl.BlockSpec(shape, lambda *_: (0,) * len(shape), pipeline_mode=pl.Buffered(1))


def _dense_ffn(h, a, w_o, g, ffn_w, in_place):
    n = h.shape[0]
    tm = TM_FFN
    row = lambda i: (i, 0)
    return pl.pallas_call(
        _ffn_kernel,
        out_shape=jax.ShapeDtypeStruct(h.shape, F32),
        grid=(n // tm,),
        in_specs=[pl.BlockSpec((tm, D_MODEL), row),
                  pl.BlockSpec((tm, D_MODEL), row),
                  _resident(w_o.shape),
                  _resident(g.shape)] + [_resident(w.shape) for w in ffn_w],
        out_specs=pl.BlockSpec((tm, D_MODEL), row),
        input_output_aliases={0: 0} if in_place else {},
        compiler_params=_params("parallel"),
        name="dense_ffn",
    )(h, a, w_o, g, *ffn_w)


def _router_kernel(h_ref, a_ref, wo_ref, g_ref, rhi_ref, rlo_ref, tri_ref,
                   h_out_ref, xn_ref, idx_ref, gate_ref, rank_ref, count_ref, base_ref):
    h = h_ref[...] + _dot(a_ref[...], wo_ref[...])
    h_out_ref[...] = h
    xn = _rms(h, g_ref[...])
    x_hi = xn.astype(BF16)
    x_lo = (xn - x_hi.astype(F32)).astype(BF16)
    xn_ref[...] = x_hi
    logits = _dot_nt(rhi_ref[...], x_hi) + _dot_nt(rhi_ref[...], x_lo) + _dot_nt(rlo_ref[...], x_hi)
    m = jnp.max(logits, axis=0, keepdims=True)
    e = jnp.exp(logits - m)
    p = e / jnp.sum(e, axis=0, keepdims=True)
    eid = lax.broadcasted_iota(jnp.int32, p.shape, 0)
    p1 = jnp.max(p, axis=0, keepdims=True)
    i1 = jnp.min(jnp.where(p == p1, eid, N_EXPERTS), axis=0, keepdims=True)
    pm = jnp.where(eid == i1, -1.0, p)
    p2 = jnp.max(pm, axis=0, keepdims=True)
    i2 = jnp.min(jnp.where(pm == p2, eid, N_EXPERTS), axis=0, keepdims=True)
    den = p1 + p2
    idx_ref[...] = jnp.concatenate([i1, i2], axis=0)
    gate_ref[...] = jnp.concatenate([p1 / den, p2 / den], axis=0)

    @pl.when(pl.program_id(0) == 0)
    def _():
        base_ref[...] = jnp.zeros(base_ref.shape, F32)

    first = (eid == i1).astype(F32)
    second = (eid == i2).astype(F32)
    earlier = tri_ref[...]
    before_first = _dot(first.astype(BF16), earlier)
    before_second = _dot(second.astype(BF16), earlier)
    n_first = jnp.sum(first, axis=1, keepdims=True)
    n_second = jnp.sum(second, axis=1, keepdims=True)
    base = base_ref[:, 0:1]
    rank_first = jnp.sum(first * (base + before_first), axis=0, keepdims=True)
    rank_second = jnp.sum(second * (base + n_first + before_second), axis=0, keepdims=True)
    rank_ref[...] = jnp.concatenate([rank_first, rank_second], axis=0).astype(jnp.int32)
    total = jnp.broadcast_to(base + n_first + n_second, base_ref.shape)
    base_ref[...] = total
    count_ref[...] = total.astype(jnp.int32)


def _router(h, a, w_o, g, r_hi, r_lo):
    n = h.shape[0]
    tm = TM_ROUTER
    row = lambda i: (i, 0)
    col = lambda i: (0, i)
    earlier = (lax.broadcasted_iota(jnp.int32, (tm, tm), 0)
               < lax.broadcasted_iota(jnp.int32, (tm, tm), 1)).astype(BF16)
    return pl.pallas_call(
        _router_kernel,
        out_shape=(jax.ShapeDtypeStruct((n, D_MODEL), F32),
                   jax.ShapeDtypeStruct((n, D_MODEL), BF16),
                   jax.ShapeDtypeStruct((TOP_K, n), jnp.int32),
                   jax.ShapeDtypeStruct((TOP_K, n), F32),
                   jax.ShapeDtypeStruct((TOP_K, n), jnp.int32),
                   jax.ShapeDtypeStruct((N_EXPERTS, LANES), jnp.int32)),
        grid=(n // tm,),
        in_specs=[pl.BlockSpec((tm, D_MODEL), row),
                  pl.BlockSpec((tm, D_MODEL), row),
                  _resident(w_o.shape),
                  _resident(g.shape),
                  _resident(r_hi.shape),
                  _resident(r_lo.shape),
                  _resident(earlier.shape)],
        out_specs=(pl.BlockSpec((tm, D_MODEL), row),
                   pl.BlockSpec((tm, D_MODEL), row),
                   pl.BlockSpec((TOP_K, tm), col),
                   pl.BlockSpec((TOP_K, tm), col),
                   pl.BlockSpec((TOP_K, tm), col),
                   pl.BlockSpec((N_EXPERTS, LANES), lambda i: (0, 0))),
        scratch_shapes=[pltpu.VMEM((N_EXPERTS, LANES), F32)],
        input_output_aliases={0: 0},
        compiler_params=_params("arbitrary"),
        name="moe_router",
    )(h, a, w_o, g, r_hi, r_lo, earlier)


def _moe_kernel(te_ref, nu_ref, x_ref, wg_ref, wu_ref, wda_ref, wdb_ref, *rest, tile0):
    o_ref = rest[-1]
    used = pl.program_id(0) + tile0 < nu_ref[0]

    @pl.when(used)
    def _():
        o_ref[...] = _swiglu(x_ref[...], (wg_ref.at[0], wu_ref.at[0], wda_ref.at[0], wdb_ref.at[0])).astype(BF16)

    @pl.when(jnp.logical_not(used))
    def _():
        o_ref[...] = jnp.zeros(o_ref.shape, BF16)


def _moe_experts(tile_expert, n_used, x_chunk, ffn_w, y_prev, tile0, p_rows):
    tm = TM_MOE
    in_specs = ([pl.BlockSpec((tm, D_MODEL), lambda t, te, nu: (t, 0))]
                + [pl.BlockSpec((1,) + w.shape[1:], lambda t, te, nu: (te[t + tile0], 0, 0)) for w in ffn_w])
    operands = [tile_expert, n_used, x_chunk, *ffn_w]
    aliases = {}
    if y_prev is not None:
        in_specs.append(pl.BlockSpec(memory_space=pl.ANY))
        operands.append(y_prev)
        aliases = {len(operands) - 1: 0}
    grid_spec = pltpu.PrefetchScalarGridSpec(
        num_scalar_prefetch=2,
        grid=(x_chunk.shape[0] // tm,),
        in_specs=in_specs,
        out_specs=pl.BlockSpec((tm, D_MODEL), lambda t, te, nu: (t + tile0, 0)),
    )
    return pl.pallas_call(
        functools.partial(_moe_kernel, tile0=tile0),
        out_shape=jax.ShapeDtypeStruct((p_rows, D_MODEL), BF16),
        grid_spec=grid_spec,
        input_output_aliases=aliases,
        compiler_params=_params("arbitrary"),
        name="moe_experts",
    )(*operands)


def _moe_dispatch(h, a, w_o, g, router, ffn_w):
    n = h.shape[0]
    tm, tr = TM_MOE, TM_ROUTER
    r_t = router.T
    r_hi = r_t.astype(BF16)
    r_lo = (r_t - r_hi.astype(F32)).astype(BF16)
    h, xn, idx, gates, rank, count = _router(h, a, w_o, g, r_hi, r_lo)

    experts = jnp.arange(N_EXPERTS, dtype=jnp.int32)
    counts = count[:, 0]
    padded = ((counts + tm - 1) // tm) * tm
    ends = jnp.cumsum(padded)
    offsets = ends - padded
    starts = jnp.cumsum(counts) - counts
    pos = rank + jnp.sum(jnp.where(idx[None] == experts[:, None, None], offsets[:, None, None], 0), axis=0)
    p_rows = TOP_K * n + N_EXPERTS * tm
    n_tiles = p_rows // tm
    n_used = (ends[-1] // tm).astype(jnp.int32)
    tile_start = jnp.arange(n_tiles, dtype=jnp.int32) * tm
    tile_expert = jnp.sum((tile_start[:, None] >= ends[None, :]).astype(jnp.int32), axis=1)
    tile_expert = jnp.minimum(tile_expert, N_EXPERTS - 1)
    last_expert = tile_expert[jnp.maximum(n_used - 1, 0)]
    tile_expert = jnp.where(jnp.arange(n_tiles) < n_used, tile_expert, last_expert)
    e_flat = idx.reshape(TOP_K, n // tr, tr).transpose(1, 0, 2).reshape(-1)
    order = jnp.argsort(e_flat, stable=True).astype(jnp.int32)
    order_tok = (order // (TOP_K * tr)) * tr + order % tr
    within = jnp.arange(p_rows, dtype=jnp.int32) - jnp.repeat(offsets[tile_expert], tm)
    src = jnp.repeat(starts[tile_expert], tm) + within
    valid = within < jnp.repeat(counts[tile_expert], tm)
    tok_sorted = jnp.where(valid, order_tok[jnp.clip(src, 0, TOP_K * n - 1)], 0)
    y_sorted = None
    chunk_tiles = n_tiles // MOE_CHUNKS
    for c in range(MOE_CHUNKS):
        rows = slice(c * chunk_tiles * tm, (c + 1) * chunk_tiles * tm)
        x_chunk = jnp.take(xn, tok_sorted[rows], axis=0, mode="clip")
        y_sorted = _moe_experts(tile_expert, n_used.reshape(1), x_chunk, ffn_w, y_sorted, c * chunk_tiles, p_rows)
    return h, y_sorted, pos, gates.T


def _mix(h_ref, y0_ref, y1_ref, gt_ref):
    gt = gt_ref[...]
    return h_ref[...] + gt[:, 0:1] * y0_ref[...].astype(F32) + gt[:, 1:2] * y1_ref[...].astype(F32)


def _gather_chunks(y_sorted, pos, gt):
    n = pos.shape[1]
    cn = n // COMBINE_CHUNKS
    for c in range(COMBINE_CHUNKS):
        rows = slice(c * cn, (c + 1) * cn)
        yield (c * cn,
               jnp.take(y_sorted, pos[0, rows], axis=0, mode="clip"),
               jnp.take(y_sorted, pos[1, rows], axis=0, mode="clip"),
               gt[rows])


def _combine_proj_kernel(h_ref, y0_ref, y1_ref, gt_ref, gk_ref, gq_ref, wkv_ref, wq_ref, *rest, q_scale):
    h_out_ref, kv_ref, q_ref = rest[-3:]
    h = _mix(h_ref, y0_ref, y1_ref, gt_ref)
    h_out_ref[...] = h
    xhat = h * lax.rsqrt(jnp.mean(h * h, axis=-1, keepdims=True) + EPS)
    kv_ref[...] = _dot((xhat * gk_ref[...]).astype(BF16), wkv_ref[...]).astype(BF16)
    q_ref[...] = (_dot((xhat * gq_ref[...]).astype(BF16), wq_ref[...]) * q_scale).astype(BF16)


def _combine_proj(h, y_sorted, pos, gt, g_kv, g_q, w_kv, w_q, q_scale):
    n = h.shape[0]
    tm = TM_PROJ
    kv = q = None
    for row0, y0, y1, gtc in _gather_chunks(y_sorted, pos, gt):
        off = row0 // tm
        row = lambda i: (i, 0)
        full = functools.partial(lambda off, i: (i + off, 0), off)
        in_specs = [pl.BlockSpec((tm, D_MODEL), full),
                    pl.BlockSpec((tm, D_MODEL), row),
                    pl.BlockSpec((tm, D_MODEL), row),
                    pl.BlockSpec((tm, TOP_K), row),
                    _resident(g_kv.shape),
                    _resident(g_q.shape),
                    _resident(w_kv.shape),
                    _resident(w_q.shape)]
        operands = [h, y0, y1, gtc, g_kv, g_q, w_kv, w_q]
        aliases = {0: 0}
        if kv is not None:
            in_specs += [pl.BlockSpec(memory_space=pl.ANY), pl.BlockSpec(memory_space=pl.ANY)]
            operands += [kv, q]
            aliases = {0: 0, 8: 1, 9: 2}
        h, kv, q = pl.pallas_call(
            functools.partial(_combine_proj_kernel, q_scale=float(q_scale)),
            out_shape=(jax.ShapeDtypeStruct(h.shape, F32),
                       jax.ShapeDtypeStruct((n, w_kv.shape[1]), BF16),
                       jax.ShapeDtypeStruct((n, w_q.shape[1]), BF16)),
            grid=(y0.shape[0] // tm,),
            in_specs=in_specs,
            out_specs=(pl.BlockSpec((tm, D_MODEL), full),
                       pl.BlockSpec((tm, w_kv.shape[1]), full),
                       pl.BlockSpec((tm, w_q.shape[1]), full)),
            input_output_aliases=aliases,
            compiler_params=_params("parallel"),
            name="combine_proj",
        )(*operands)
    return h, kv, q


def _combine_norm_kernel(h_ref, y0_ref, y1_ref, gt_ref, g_ref, o_ref):
    o_ref[...] = _rms(_mix(h_ref, y0_ref, y1_ref, gt_ref), g_ref[...])


def _combine_norm(h, y_sorted, pos, gt, g):
    tm = TM_PROJ
    for row0, y0, y1, gtc in _gather_chunks(y_sorted, pos, gt):
        off = row0 // tm
        row = lambda i: (i, 0)
        full = functools.partial(lambda off, i: (i + off, 0), off)
        h = pl.pallas_call(
            _combine_norm_kernel,
            out_shape=jax.ShapeDtypeStruct(h.shape, F32),
            grid=(y0.shape[0] // tm,),
            in_specs=[pl.BlockSpec((tm, D_MODEL), full),
                      pl.BlockSpec((tm, D_MODEL), row),
                      pl.BlockSpec((tm, D_MODEL), row),
                      pl.BlockSpec((tm, TOP_K), row),
                      _resident(g.shape)],
            out_specs=pl.BlockSpec((tm, D_MODEL), full),
            input_output_aliases={0: 0},
            compiler_params=_params("parallel"),
            name="combine_norm",
        )(h, y0, y1, gtc, g)
    return h


def _sb_attn_kernel(q_ref, k_ref, v_ref, u_ref, o_ref, *, tb, heads):
    tri = u_ref[...]
    nq = q_ref.shape[1] // tb
    below_diag = (lax.broadcasted_iota(jnp.int32, (tb, tb), 1)
                  < lax.broadcasted_iota(jnp.int32, (tb, tb), 0))

    def block(q, g, jb, r_sum, acc, diagonal):
        start = _aligned(jb * tb, tb)
        kb = k_ref[0, pl.ds(start, tb), g * SB_HEAD:(g + 1) * SB_HEAD]
        vb = v_ref[0, pl.ds(start, tb), g * SB_HEAD:(g + 1) * SB_HEAD]
        z = _dot_nt(q, kb)
        neg_abs = pltpu.bitcast(pltpu.bitcast(z, jnp.uint32) | jnp.uint32(0x80000000), F32)
        soft = jnp.log(1.0 + jnp.exp2(neg_abs)) * LOG2E
        ls = jnp.minimum(z, 0.0) - soft
        lb = ls - z
        if diagonal:
            lb = jnp.where(below_diag, lb, 0.0)
        between = _dot(lb.astype(BF16), tri)
        a = jnp.exp2(ls + between + r_sum)
        if diagonal:
            a = jnp.where(below_diag, a, 0.0)
        acc = acc + _dot(a.astype(BF16), vb)
        r_sum = r_sum + jnp.sum(lb, axis=-1, keepdims=True)
        return r_sum, acc

    def q_block(i, first):
        qs = _aligned(i * tb, tb)
        qg = [q_ref[0, pl.ds(qs, tb), g * SB_HEAD:(g + 1) * SB_HEAD] for g in range(heads)]
        zero_r, zero_acc = jnp.zeros((tb, 1), F32), jnp.zeros((tb, SB_HEAD), F32)
        state = [block(qg[g], g, i, zero_r, zero_acc, True) for g in range(heads)]
        if not first:
            state = [block(qg[g], g, i - 1, state[g][0], state[g][1], False) for g in range(heads)]

            def cond(c):
                live = jnp.max(c[1][0][0])
                for g in range(1, heads):
                    live = jnp.maximum(live, jnp.max(c[1][g][0]))
                return jnp.logical_and(c[0] >= 0, live > SB_EXIT_LOG2)

            def body(c):
                return c[0] - 1, [block(qg[g], g, c[0], c[1][g][0], c[1][g][1], False) for g in range(heads)]

            _, state = lax.while_loop(cond, body, (i - 2, state))
        for g in range(heads):
            o_ref[0, pl.ds(qs, tb), g * SB_HEAD:(g + 1) * SB_HEAD] = state[g][1].astype(BF16)

    q_block(0, True)

    def rest(i, carry):
        q_block(i, False)
        return carry

    lax.fori_loop(1, nq, rest, 0)


def _sb_attn(q, kv, tri):
    b, s, _ = q.shape
    hg = SB_HEADS_PER_STEP
    ng = SB_HEADS // hg
    w = hg * SB_HEAD
    return pl.pallas_call(
        functools.partial(_sb_attn_kernel, tb=TB_SB, heads=hg),
        out_shape=jax.ShapeDtypeStruct((b, s, SB_HEADS * SB_HEAD), BF16),
        grid=(b, ng),
        in_specs=[pl.BlockSpec((1, s, w), lambda bi, h: (bi, 0, h)),
                  pl.BlockSpec((1, s, w), lambda bi, h: (bi, 0, h)),
                  pl.BlockSpec((1, s, w), lambda bi, h: (bi, 0, ng + h)),
                  pl.BlockSpec((TB_SB, TB_SB), lambda bi, h: (0, 0))],
        out_specs=pl.BlockSpec((1, s, w), lambda bi, h: (bi, 0, h)),
        compiler_params=_params("parallel", "parallel"),
        name="sb_attn",
    )(q, kv, kv, tri)


def _rope_tables(positions):
    inv_freq = 1.0 / (ROPE_BASE ** (jnp.arange(0, QK_ROPE, 2, dtype=F32) / QK_ROPE))
    ang = positions.astype(F32).reshape(-1, 1) * inv_freq
    cos, sin = jnp.cos(ang), jnp.sin(ang)
    z32 = jnp.zeros_like(cos)
    z64 = jnp.zeros((cos.shape[0], LANES - QK_ROPE), F32)
    cos_t = jnp.concatenate([cos, cos, z64], axis=-1)
    sin_a = jnp.concatenate([-sin, z32, z64], axis=-1)
    sin_b = jnp.concatenate([z32, sin, z64], axis=-1)
    return cos_t, sin_a, sin_b


def _mla_weights(w_down, w_uq, w_ukv):
    wd = jnp.pad(w_down, ((0, 0), (0, DOWN_PAD - w_down.shape[1]))).astype(BF16)
    wuq = w_uq.reshape(Q_LORA, MLA_HEADS, QK_NOPE + QK_ROPE)
    wuq = jnp.pad(wuq, ((0, 0), (0, 0), (0, MLA_QK_PAD - QK_NOPE - QK_ROPE)))
    wuq = wuq.reshape(Q_LORA, MLA_HEADS * MLA_QK_PAD).astype(BF16)
    wukv = w_ukv.reshape(KV_LORA, MLA_HEADS, QK_NOPE + V_HEAD)
    wuk = wukv[:, :, :QK_NOPE].reshape(KV_LORA, MLA_HEADS * QK_NOPE).astype(BF16)
    wuv = wukv[:, :, QK_NOPE:].reshape(KV_LORA, MLA_HEADS * V_HEAD).astype(BF16)
    return wd, wuq, wuk, wuv


def kernel(x, positions, attn_norm_0, mla_w_down_0, mla_q_norm_0, mla_w_uq_0, mla_kv_norm_0, mla_w_ukv_0, mla_w_o_0, ffn_norm_0, ffn_w_gate_up_0, ffn_w_down_0, attn_norm_1, mla_w_down_1, mla_q_norm_1, mla_w_uq_1, mla_kv_norm_1, mla_w_ukv_1, mla_w_o_1, ffn_norm_1, moe_router_1, moe_w_gate_up_1, moe_w_down_1, kv_shared_norm, kv_shared_w, attn_norm_2, sb_w_q_2, sb_w_o_2, ffn_norm_2, ffn_w_gate_up_2, ffn_w_down_2, attn_norm_3, sb_w_q_3, sb_w_o_3, ffn_norm_3, moe_router_3, moe_w_gate_up_3, moe_w_down_3, final_norm):
    b, s, d = x.shape
    n = b * s
    row = lambda g: g.reshape(1, -1)
    h = x.reshape(n, d)
    cos_t, sin_a, sin_b = _rope_tables(positions)

    mla_layers = [
        (attn_norm_0, mla_w_down_0, mla_q_norm_0, mla_w_uq_0, mla_kv_norm_0, mla_w_ukv_0, mla_w_o_0),
        (attn_norm_1, mla_w_down_1, mla_q_norm_1, mla_w_uq_1, mla_kv_norm_1, mla_w_ukv_1, mla_w_o_1),
    ]
    tri = (lax.broadcasted_iota(jnp.int32, (TB_SB, TB_SB), 0)
           > lax.broadcasted_iota(jnp.int32, (TB_SB, TB_SB), 1)).astype(BF16)
    sb_scale = SB_HEAD ** -0.5 * LOG2E

    def ffn_weights(w_gate_up, w_down):
        return _to_bf16_halves(w_gate_up, 4) + _to_bf16_halves(w_down, 2)

    def mla_attention(h, layer, cast_gate_up, cast_down):
        an, w_down, qn, w_uq, kvn, w_ukv, _ = mla_layers[layer]
        wd, wuq, wuk, wuv = _mla_weights(w_down, w_uq, w_ukv)
        q, k, v = _mla_proj(h, row(an), wd, row(qn), row(kvn), wuq, wuk, wuv, cos_t, sin_a, sin_b)
        o, cast = _mla_attn(q.reshape(b, s, -1), k.reshape(b, s, -1), v.reshape(b, s, -1), cast_gate_up, cast_down)
        return o.reshape(n, -1), cast

    o, moe_w_1 = mla_attention(h, 0, moe_w_gate_up_1, moe_w_down_1)
    h = _dense_ffn(h, o, mla_w_o_0.astype(BF16), row(ffn_norm_0), ffn_weights(ffn_w_gate_up_0, ffn_w_down_0), False)
    o, moe_w_3 = mla_attention(h, 1, moe_w_gate_up_3, moe_w_down_3)
    h, y_sorted, pos, gt = _moe_dispatch(h, o, mla_w_o_1.astype(BF16), row(ffn_norm_1), moe_router_1, moe_w_1)
    h, kv_shared, q = _combine_proj(h, y_sorted, pos, gt, row(kv_shared_norm), row(attn_norm_2),
                                    kv_shared_w.astype(BF16), sb_w_q_2.astype(BF16), sb_scale)
    kv_shared = kv_shared.reshape(b, s, -1)
    o = _sb_attn(q.reshape(b, s, -1), kv_shared, tri).reshape(n, -1)
    h = _dense_ffn(h, o, sb_w_o_2.astype(BF16), row(ffn_norm_2), ffn_weights(ffn_w_gate_up_2, ffn_w_down_2), True)
    q = _norm_matmul(h, row(attn_norm_3), sb_w_q_3.astype(BF16), scale=sb_scale)
    o = _sb_attn(q.reshape(b, s, -1), kv_shared, tri).reshape(n, -1)
    h, y_sorted, pos, gt = _moe_dispatch(h, o, sb_w_o_3.astype(BF16), row(ffn_norm_3), moe_router_3, moe_w_3)
    return _combine_norm(h, y_sorted, pos, gt, row(final_norm)).reshape(b, s, d)
```

```python
import functools

import jax
import jax.numpy as jnp
from jax import lax
from jax.experimental import pallas as pl
from jax.experimental.pallas import tpu as pltpu

F32 = jnp.float32
BF16 = jnp.bfloat16

D_MODEL = 1024
CHUNK = 64
MLA_HEADS = 8
QK_NOPE = 128
QK_ROPE = 64
V_HEAD = 128
Q_LORA = 384
KV_LORA = 256
ROPE_BASE = 10000.0
SB_HEADS = 8
SB_HEAD = 128
D_FF = 2816
N_EXPERTS = 8
TOP_K = 2
EPS = 1e-6

LANES = 128
MLA_QK_PAD = 256
DOWN_PAD = 768
VMEM_LIMIT = 56 * 1024 * 1024

TM_PROJ = 512
TM_FFN = 512
TM_MOE = 256
MOE_CHUNKS = 4
COMBINE_CHUNKS = 4
CAST_BLOCK_BYTES = 12 * 1024 * 1024
TQ_MLA = 512
TK_MLA = 512
MLA_ROW_CHUNK = 32
MLA_HEADS_PER_STEP = 4
TB_SB = 256
SB_HEADS_PER_STEP = 4
TM_ROUTER = 512
LOG2E = 1.4426950408889634
SB_EXIT_LOG2 = -150.0


def _rms(x, g):
    ms = jnp.mean(x * x, axis=-1, keepdims=True)
    return x * lax.rsqrt(ms + EPS) * g


def _dot(a, b):
    return jnp.dot(a, b, preferred_element_type=F32)


def _dot_nt(a, b):
    return lax.dot_general(a, b, (((1,), (1,)), ((), ())), preferred_element_type=F32)


def _silu(g):
    return g / (1.0 + jnp.exp(-g))


def _aligned(x, m):
    return x if isinstance(x, int) else pl.multiple_of(x, m)


def _params(*sem):
    return pltpu.CompilerParams(dimension_semantics=sem, vmem_limit_bytes=VMEM_LIMIT)


def _mla_proj_kernel(h_ref, g_ref, wd_ref, qn_ref, kvn_ref, wuq_ref, wuk_ref, wuv_ref,
                     trig_ref, q_ref, k_ref, v_ref, *, scale):
    xn = _rms(h_ref[...], g_ref[...]).astype(BF16)
    c = _dot(xn, wd_ref[...])
    cq = _rms(c[:, :Q_LORA], qn_ref[...]).astype(BF16)
    ckv = _rms(c[:, Q_LORA:Q_LORA + KV_LORA], kvn_ref[...]).astype(BF16)
    trig = trig_ref[...]
    swapped = pltpu.roll(trig, QK_ROPE, 1)
    lane = lax.broadcasted_iota(jnp.int32, trig.shape, 1)
    cos_t = jnp.where(lane < QK_ROPE, trig, 0.0)
    sin_a = jnp.where(lane < QK_ROPE // 2, -swapped, 0.0)
    sin_b = jnp.where(jnp.logical_and(lane >= QK_ROPE // 2, lane < QK_ROPE), swapped, 0.0)

    def rope(r):
        return r * cos_t + pltpu.roll(r, 96, 1) * sin_a + pltpu.roll(r, 32, 1) * sin_b

    kr = rope(c[:, Q_LORA + KV_LORA:]).astype(BF16)
    q = _dot(cq, wuq_ref[...])
    kn = _dot(ckv, wuk_ref[...])
    v_ref[...] = _dot(ckv, wuv_ref[...]).astype(BF16)
    for h in range(MLA_HEADS):
        lo = h * MLA_QK_PAD
        q_ref[:, lo:lo + LANES] = (q[:, lo:lo + LANES] * scale).astype(BF16)
        q_ref[:, lo + LANES:lo + 2 * LANES] = (rope(q[:, lo + LANES:lo + 2 * LANES]) * scale).astype(BF16)
        k_ref[:, lo:lo + LANES] = kn[:, h * LANES:(h + 1) * LANES].astype(BF16)
        k_ref[:, lo + LANES:lo + 2 * LANES] = kr


def _mla_proj(h, g, wd, qn, kvn, wuq, wuk, wuv, trig):
    n = h.shape[0]
    tm = TM_PROJ
    row = lambda i: (i, 0)
    fixed = lambda i: (0, 0)
    scale = float((QK_NOPE + QK_ROPE) ** -0.5 * LOG2E)
    return pl.pallas_call(
        functools.partial(_mla_proj_kernel, scale=scale),
        out_shape=(jax.ShapeDtypeStruct((n, MLA_HEADS * MLA_QK_PAD), BF16),
                   jax.ShapeDtypeStruct((n, MLA_HEADS * MLA_QK_PAD), BF16),
                   jax.ShapeDtypeStruct((n, MLA_HEADS * V_HEAD), BF16)),
        grid=(n // tm,),
        in_specs=[pl.BlockSpec((tm, D_MODEL), row),
                  pl.BlockSpec((1, D_MODEL), fixed),
                  pl.BlockSpec((D_MODEL, DOWN_PAD), fixed),
                  pl.BlockSpec((1, Q_LORA), fixed),
                  pl.BlockSpec((1, KV_LORA), fixed),
                  pl.BlockSpec((Q_LORA, MLA_HEADS * MLA_QK_PAD), fixed),
                  pl.BlockSpec((KV_LORA, MLA_HEADS * QK_NOPE), fixed),
                  pl.BlockSpec((KV_LORA, MLA_HEADS * V_HEAD), fixed),
                  pl.BlockSpec((tm, LANES), row)],
        out_specs=(pl.BlockSpec((tm, MLA_HEADS * MLA_QK_PAD), row),
                   pl.BlockSpec((tm, MLA_HEADS * MLA_QK_PAD), row),
                   pl.BlockSpec((tm, MLA_HEADS * V_HEAD), row)),
        compiler_params=_params("parallel"),
        name="mla_proj",
    )(h, g, wd, qn, kvn, wuq, wuk, wuv, trig)


def _mla_attn_kernel(q_ref, k_ref, v_ref, wgu_ref, wd_ref, o_ref, wg_ref, wu_ref, wda_ref, wdb_ref,
                     s_ref, p_ref, m_ref, l_ref, alpha_ref, acc_ref, *, tq, tk, heads):
    wg_ref[...] = wgu_ref[:, :D_FF].astype(BF16)
    wu_ref[...] = wgu_ref[:, D_FF:].astype(BF16)
    wda_ref[...] = wd_ref[:, :D_MODEL // 2].astype(BF16)
    wdb_ref[...] = wd_ref[:, D_MODEL // 2:].astype(BF16)

    i = pl.program_id(2)
    qg = [q_ref[0, :, g * MLA_QK_PAD:(g + 1) * MLA_QK_PAD] for g in range(heads)]
    rc = MLA_ROW_CHUNK

    m_ref[...] = jnp.full(m_ref.shape, -jnp.inf, F32)
    l_ref[...] = jnp.zeros(l_ref.shape, F32)
    acc_ref[...] = jnp.zeros(acc_ref.shape, F32)

    def scores(g, j):
        kb = k_ref[0, pl.ds(_aligned(j * tk, tk), tk), g * MLA_QK_PAD:(g + 1) * MLA_QK_PAD]
        s_ref[g] = _dot_nt(qg[g], kb)

    def update(g, j, diagonal):
        vb = v_ref[0, pl.ds(_aligned(j * tk, tk), tk), g * V_HEAD:(g + 1) * V_HEAD]
        for c in range(tq // rc):
            rows = slice(c * rc, (c + 1) * rc)
            live_tiles = tk // LANES
            if diagonal:
                visible = (((c + 1) * rc - 1) // CHUNK + 1) * CHUNK
                live_tiles = -(-visible // LANES)
                s = s_ref[g, rows, :live_tiles * LANES]
                qc = (lax.broadcasted_iota(jnp.int32, s.shape, 0) + c * rc) // CHUNK
                kc = lax.broadcasted_iota(jnp.int32, s.shape, 1) // CHUNK
                s = jnp.where(kc <= qc, s, -jnp.inf)
                if live_tiles < tk // LANES:
                    p_ref[g, rows, live_tiles * LANES:] = jnp.zeros((rc, tk - live_tiles * LANES), BF16)
            else:
                s = s_ref[g, rows, :]
            m_old = m_ref[g, rows, :]
            m_new = jnp.maximum(m_old, jnp.max(s, axis=-1, keepdims=True))
            alpha = jnp.exp2(m_old - m_new)
            m_ref[g, rows, :] = m_new
            alpha_ref[g, rows, :] = alpha
            psum = jnp.zeros((rc, LANES), F32)
            for t in range(live_tiles):
                p = jnp.exp2(s[:, t * LANES:(t + 1) * LANES] - m_new)
                psum = psum + p
                p_ref[g, rows, t * LANES:(t + 1) * LANES] = p.astype(BF16)
            l_ref[g, rows, :] = alpha * l_ref[g, rows, :] + jnp.sum(psum, axis=-1, keepdims=True)
        acc_ref[g] = alpha_ref[g] * acc_ref[g] + _dot(p_ref[g], vb)

    for g in range(heads):
        scores(g, 0)

    def body(j, carry):
        for g in range(heads):
            update(g, j, False)
            scores(g, j + 1)
        return carry

    lax.fori_loop(0, i, body, 0)
    for g in range(heads):
        update(g, i, True)
    for g in range(heads):
        o_ref[0, :, g * V_HEAD:(g + 1) * V_HEAD] = (acc_ref[g] / l_ref[g]).astype(BF16)


def _mla_attn(q, k, v, w_gate_up, w_down):
    b, s, _ = q.shape
    tq, tk, hg = TQ_MLA, TK_MLA, MLA_HEADS_PER_STEP
    ng, nq = MLA_HEADS // hg, s // tq
    steps = b * ng * nq
    wgu2 = w_gate_up.reshape(-1, 2 * D_FF)
    wd2 = w_down.reshape(-1, D_MODEL)
    r_gu, r_d = wgu2.shape[0] // steps, wd2.shape[0] // steps
    assert r_gu * steps == wgu2.shape[0] and r_d * steps == wd2.shape[0] and r_gu % 16 == 0 and r_d % 16 == 0
    slab = lambda bi, h, i: ((bi * ng + h) * nq + i, 0)
    o, wg, wu, wda, wdb = pl.pallas_call(
        functools.partial(_mla_attn_kernel, tq=tq, tk=tk, heads=hg),
        out_shape=(jax.ShapeDtypeStruct((b, s, MLA_HEADS * V_HEAD), BF16),
                   jax.ShapeDtypeStruct((wgu2.shape[0], D_FF), BF16),
                   jax.ShapeDtypeStruct((wgu2.shape[0], D_FF), BF16),
                   jax.ShapeDtypeStruct((wd2.shape[0], D_MODEL // 2), BF16),
                   jax.ShapeDtypeStruct((wd2.shape[0], D_MODEL // 2), BF16)),
        grid=(b, ng, nq),
        in_specs=[pl.BlockSpec((1, tq, hg * MLA_QK_PAD), lambda bi, h, i: (bi, i, h)),
                  pl.BlockSpec((1, s, hg * MLA_QK_PAD), lambda bi, h, i: (bi, 0, h)),
                  pl.BlockSpec((1, s, hg * V_HEAD), lambda bi, h, i: (bi, 0, h)),
                  pl.BlockSpec((r_gu, 2 * D_FF), slab),
                  pl.BlockSpec((r_d, D_MODEL), slab)],
        out_specs=(pl.BlockSpec((1, tq, hg * V_HEAD), lambda bi, h, i: (bi, i, h)),
                   pl.BlockSpec((r_gu, D_FF), slab),
                   pl.BlockSpec((r_gu, D_FF), slab),
                   pl.BlockSpec((r_d, D_MODEL // 2), slab),
                   pl.BlockSpec((r_d, D_MODEL // 2), slab)),
        scratch_shapes=[pltpu.VMEM((hg, tq, tk), F32), pltpu.VMEM((hg, tq, tk), BF16),
                        pltpu.VMEM((hg, tq, LANES), F32), pltpu.VMEM((hg, tq, LANES), F32),
                        pltpu.VMEM((hg, tq, LANES), F32), pltpu.VMEM((hg, tq, V_HEAD), F32)],
        compiler_params=_params("parallel", "parallel", "arbitrary"),
        name="mla_attn",
    )(q, k, v, wgu2, wd2)
    e = w_gate_up.shape[0]
    ffn_w = (wg.reshape(e, D_MODEL, D_FF), wu.reshape(e, D_MODEL, D_FF),
             wda.reshape(e, D_FF, D_MODEL // 2), wdb.reshape(e, D_FF, D_MODEL // 2))
    return o, ffn_w


def _cast_kernel(*refs, n_in, n_out):
    per = n_in // n_out
    for o in range(n_out):
        parts = [refs[o * per + k][...].astype(BF16) for k in range(per)]
        refs[n_in + o][...] = parts[0] if per == 1 else jnp.concatenate(parts, axis=1)


def _to_bf16_halves(w, n_in):
    n_out = 2
    cols = w.shape[-1]
    rows = w.size // cols
    wi, wo = cols // n_in, cols // n_out
    tr = max(r for r in range(16, rows + 1, 16) if rows % r == 0 and r * cols * 4 <= CAST_BLOCK_BYTES)
    outs = pl.pallas_call(
        functools.partial(_cast_kernel, n_in=n_in, n_out=n_out),
        out_shape=tuple(jax.ShapeDtypeStruct((rows, wo), BF16) for _ in range(n_out)),
        grid=(rows // tr,),
        in_specs=[pl.BlockSpec((tr, wi), functools.partial(lambda c, i: (i, c), c)) for c in range(n_in)],
        out_specs=tuple(pl.BlockSpec((tr, wo), lambda i: (i, 0)) for _ in range(n_out)),
        compiler_params=_params("parallel"),
        name="cast_bf16",
    )(*([w.reshape(rows, cols)] * n_in))
    return tuple(o.reshape(w.shape[:-1] + (wo,)) for o in outs)


def _swiglu(x, w_refs):
    wg_ref, wu_ref, wda_ref, wdb_ref = w_refs
    gate = _dot(x, wg_ref[...])
    up = _dot(x, wu_ref[...])
    act = (_silu(gate) * up).astype(BF16)
    return jnp.concatenate([_dot(act, wda_ref[...]), _dot(act, wdb_ref[...])], axis=1)


def _ffn_kernel(h_ref, a_ref, wo_ref, g_ref, wg_ref, wu_ref, wda_ref, wdb_ref, *rest, q_scale):
    h = h_ref[...] + _dot(a_ref[...], wo_ref[...])
    x = _rms(h, g_ref[...]).astype(BF16)
    h = h + _swiglu(x, (wg_ref, wu_ref, wda_ref, wdb_ref))
    if q_scale is None:
        (o_ref,) = rest
    else:
        gq_ref, wq_ref, o_ref, q_ref = rest
        q_ref[...] = (_dot(_rms(h, gq_ref[...]).astype(BF16), wq_ref[...]) * q_scale).astype(BF16)
    o_ref[...] = h


def _resident(shape):
    return pl.BlockSpec(shape, lambda *_: (0,) * len(shape), pipeline_mode=pl.Buffered(1))


def _dense_ffn(h, a, w_o, g, ffn_w, in_place, q_proj=None):
    n = h.shape[0]
    tm = TM_FFN
    row = lambda i: (i, 0)
    in_specs = [pl.BlockSpec((tm, D_MODEL), row),
                pl.BlockSpec((tm, D_MODEL), row),
                _resident(w_o.shape),
                _resident(g.shape)] + [_resident(w.shape) for w in ffn_w]
    operands = [h, a, w_o, g, *ffn_w]
    out_shape = [jax.ShapeDtypeStruct(h.shape, F32)]
    out_specs = [pl.BlockSpec((tm, D_MODEL), row)]
    q_scale = None
    if q_proj is not None:
        g_q, w_q, q_scale = q_proj
        in_specs += [_resident(g_q.shape), _resident(w_q.shape)]
        operands += [g_q, w_q]
        out_shape.append(jax.ShapeDtypeStruct((n, w_q.shape[1]), BF16))
        out_specs.append(pl.BlockSpec((tm, w_q.shape[1]), row))
        q_scale = float(q_scale)
    out = pl.pallas_call(
        functools.partial(_ffn_kernel, q_scale=q_scale),
        out_shape=tuple(out_shape),
        grid=(n // tm,),
        in_specs=in_specs,
        out_specs=tuple(out_specs),
        input_output_aliases={0: 0} if in_place else {},
        compiler_params=_params("parallel"),
        name="dense_ffn",
    )(*operands)
    return out[0] if q_proj is None else out


def _router_kernel(h_ref, a_ref, wo_ref, g_ref, rhi_ref, rlo_ref, tri_ref,
                   h_out_ref, xn_ref, idx_ref, gate_ref, rank_ref, count_ref, base_ref):
    h = h_ref[...] + _dot(a_ref[...], wo_ref[...])
    h_out_ref[...] = h
    xn = _rms(h, g_ref[...])
    x_hi = xn.astype(BF16)
    x_lo = (xn - x_hi.astype(F32)).astype(BF16)
    xn_ref[...] = x_hi
    logits = _dot_nt(rhi_ref[...], x_hi) + _dot_nt(rhi_ref[...], x_lo) + _dot_nt(rlo_ref[...], x_hi)
    m = jnp.max(logits, axis=0, keepdims=True)
    e = jnp.exp(logits - m)
    p = e / jnp.sum(e, axis=0, keepdims=True)
    eid = lax.broadcasted_iota(jnp.int32, p.shape, 0)
    p1 = jnp.max(p, axis=0, keepdims=True)
    i1 = jnp.min(jnp.where(p == p1, eid, N_EXPERTS), axis=0, keepdims=True)
    pm = jnp.where(eid == i1, -1.0, p)
    p2 = jnp.max(pm, axis=0, keepdims=True)
    i2 = jnp.min(jnp.where(pm == p2, eid, N_EXPERTS), axis=0, keepdims=True)
    den = p1 + p2
    idx_ref[...] = jnp.concatenate([i1, i2], axis=0)
    gate_ref[...] = jnp.concatenate([p1 / den, p2 / den], axis=0)

    @pl.when(pl.program_id(0) == 0)
    def _():
        base_ref[...] = jnp.zeros(base_ref.shape, F32)

    first = (eid == i1).astype(F32)
    second = (eid == i2).astype(F32)
    earlier = tri_ref[...]
    before_first = _dot(first.astype(BF16), earlier)
    before_second = _dot(second.astype(BF16), earlier)
    n_first = jnp.sum(first, axis=1, keepdims=True)
    n_second = jnp.sum(second, axis=1, keepdims=True)
    base = base_ref[:, 0:1]
    rank_first = jnp.sum(first * (base + before_first), axis=0, keepdims=True)
    rank_second = jnp.sum(second * (base + n_first + before_second), axis=0, keepdims=True)
    rank_ref[...] = jnp.concatenate([rank_first, rank_second], axis=0).astype(jnp.int32)
    total = jnp.broadcast_to(base + n_first + n_second, base_ref.shape)
    base_ref[...] = total
    count_ref[...] = total.astype(jnp.int32)


def _router(h, a, w_o, g, r_hi, r_lo):
    n = h.shape[0]
    tm = TM_ROUTER
    row = lambda i: (i, 0)
    col = lambda i: (0, i)
    earlier = (lax.broadcasted_iota(jnp.int32, (tm, tm), 0)
               < lax.broadcasted_iota(jnp.int32, (tm, tm), 1)).astype(BF16)
    return pl.pallas_call(
        _router_kernel,
        out_shape=(jax.ShapeDtypeStruct((n, D_MODEL), F32),
                   jax.ShapeDtypeStruct((n, D_MODEL), BF16),
                   jax.ShapeDtypeStruct((TOP_K, n), jnp.int32),
                   jax.ShapeDtypeStruct((TOP_K, n), F32),
                   jax.ShapeDtypeStruct((TOP_K, n), jnp.int32),
                   jax.ShapeDtypeStruct((N_EXPERTS, LANES), jnp.int32)),
        grid=(n // tm,),
        in_specs=[pl.BlockSpec((tm, D_MODEL), row),
                  pl.BlockSpec((tm, D_MODEL), row),
                  _resident(w_o.shape),
                  _resident(g.shape),
                  _resident(r_hi.shape),
                  _resident(r_lo.shape),
                  _resident(earlier.shape)],
        out_specs=(pl.BlockSpec((tm, D_MODEL), row),
                   pl.BlockSpec((tm, D_MODEL), row),
                   pl.BlockSpec((TOP_K, tm), col),
                   pl.BlockSpec((TOP_K, tm), col),
                   pl.BlockSpec((TOP_K, tm), col),
                   pl.BlockSpec((N_EXPERTS, LANES), lambda i: (0, 0))),
        scratch_shapes=[pltpu.VMEM((N_EXPERTS, LANES), F32)],
        input_output_aliases={0: 0},
        compiler_params=_params("arbitrary"),
        name="moe_router",
    )(h, a, w_o, g, r_hi, r_lo, earlier)


def _moe_kernel(te_ref, nu_ref, x_ref, wg_ref, wu_ref, wda_ref, wdb_ref, *rest, tile0):
    o_ref = rest[-1]
    used = pl.program_id(0) + tile0 < nu_ref[0]

    @pl.when(used)
    def _():
        o_ref[...] = _swiglu(x_ref[...], (wg_ref.at[0], wu_ref.at[0], wda_ref.at[0], wdb_ref.at[0])).astype(BF16)

    @pl.when(jnp.logical_not(used))
    def _():
        o_ref[...] = jnp.zeros(o_ref.shape, BF16)


def _moe_experts(tile_expert, n_used, x_chunk, ffn_w, y_prev, tile0, p_rows):
    tm = TM_MOE
    in_specs = ([pl.BlockSpec((tm, D_MODEL), lambda t, te, nu: (t, 0))]
                + [pl.BlockSpec((1,) + w.shape[1:], lambda t, te, nu: (te[t + tile0], 0, 0)) for w in ffn_w])
    operands = [tile_expert, n_used, x_chunk, *ffn_w]
    aliases = {}
    if y_prev is not None:
        in_specs.append(pl.BlockSpec(memory_space=pl.ANY))
        operands.append(y_prev)
        aliases = {len(operands) - 1: 0}
    grid_spec = pltpu.PrefetchScalarGridSpec(
        num_scalar_prefetch=2,
        grid=(x_chunk.shape[0] // tm,),
        in_specs=in_specs,
        out_specs=pl.BlockSpec((tm, D_MODEL), lambda t, te, nu: (t + tile0, 0)),
    )
    return pl.pallas_call(
        functools.partial(_moe_kernel, tile0=tile0),
        out_shape=jax.ShapeDtypeStruct((p_rows, D_MODEL), BF16),
        grid_spec=grid_spec,
        input_output_aliases=aliases,
        compiler_params=_params("arbitrary"),
        name="moe_experts",
    )(*operands)


def _moe_dispatch(h, a, w_o, g, router, ffn_w):
    n = h.shape[0]
    tm, tr = TM_MOE, TM_ROUTER
    r_t = router.T
    r_hi = r_t.astype(BF16)
    r_lo = (r_t - r_hi.astype(F32)).astype(BF16)
    h, xn, idx, gates, rank, count = _router(h, a, w_o, g, r_hi, r_lo)

    experts = jnp.arange(N_EXPERTS, dtype=jnp.int32)
    counts = count[:, 0]
    padded = ((counts + tm - 1) // tm) * tm
    ends = jnp.cumsum(padded)
    offsets = ends - padded
    starts = jnp.cumsum(counts) - counts
    pos = rank + jnp.sum(jnp.where(idx[None] == experts[:, None, None], offsets[:, None, None], 0), axis=0)
    p_rows = TOP_K * n + N_EXPERTS * tm
    n_tiles = p_rows // tm
    n_used = (ends[-1] // tm).astype(jnp.int32)
    tile_start = jnp.arange(n_tiles, dtype=jnp.int32) * tm
    tile_expert = jnp.sum((tile_start[:, None] >= ends[None, :]).astype(jnp.int32), axis=1)
    tile_expert = jnp.minimum(tile_expert, N_EXPERTS - 1)
    last_expert = tile_expert[jnp.maximum(n_used - 1, 0)]
    tile_expert = jnp.where(jnp.arange(n_tiles) < n_used, tile_expert, last_expert)
    e_flat = idx.reshape(TOP_K, n // tr, tr).transpose(1, 0, 2).reshape(-1)
    order = jnp.argsort(e_flat, stable=True).astype(jnp.int32)
    order_tok = (order // (TOP_K * tr)) * tr + order % tr
    within = jnp.arange(p_rows, dtype=jnp.int32) - jnp.repeat(offsets[tile_expert], tm)
    src = jnp.repeat(starts[tile_expert], tm) + within
    valid = within < jnp.repeat(counts[tile_expert], tm)
    tok_sorted = jnp.where(valid, order_tok[jnp.clip(src, 0, TOP_K * n - 1)], 0)
    y_sorted = None
    chunk_tiles = n_tiles // MOE_CHUNKS
    for c in range(MOE_CHUNKS):
        rows = slice(c * chunk_tiles * tm, (c + 1) * chunk_tiles * tm)
        x_chunk = jnp.take(xn, tok_sorted[rows], axis=0, mode="clip")
        y_sorted = _moe_experts(tile_expert, n_used.reshape(1), x_chunk, ffn_w, y_sorted, c * chunk_tiles, p_rows)
    return h, y_sorted, pos, gates.T


def _mix(h_ref, y0_ref, y1_ref, gt_ref):
    gt = gt_ref[...]
    return h_ref[...] + gt[:, 0:1] * y0_ref[...].astype(F32) + gt[:, 1:2] * y1_ref[...].astype(F32)


def _gather_chunks(y_sorted, pos, gt):
    n = pos.shape[1]
    cn = n // COMBINE_CHUNKS
    for c in range(COMBINE_CHUNKS):
        rows = slice(c * cn, (c + 1) * cn)
        yield (c * cn,
               jnp.take(y_sorted, pos[0, rows], axis=0, mode="clip"),
               jnp.take(y_sorted, pos[1, rows], axis=0, mode="clip"),
               gt[rows])


def _combine_proj_kernel(h_ref, y0_ref, y1_ref, gt_ref, gk_ref, gq_ref, wkv_ref, wq_ref, *rest, q_scale):
    h_out_ref, kv_ref, q_ref = rest[-3:]
    h = _mix(h_ref, y0_ref, y1_ref, gt_ref)
    h_out_ref[...] = h
    xhat = h * lax.rsqrt(jnp.mean(h * h, axis=-1, keepdims=True) + EPS)
    kv_ref[...] = _dot((xhat * gk_ref[...]).astype(BF16), wkv_ref[...]).astype(BF16)
    q_ref[...] = (_dot((xhat * gq_ref[...]).astype(BF16), wq_ref[...]) * q_scale).astype(BF16)


def _combine_proj(h, y_sorted, pos, gt, g_kv, g_q, w_kv, w_q, q_scale):
    n = h.shape[0]
    tm = TM_PROJ
    kv = q = None
    for row0, y0, y1, gtc in _gather_chunks(y_sorted, pos, gt):
        off = row0 // tm
        row = lambda i: (i, 0)
        full = functools.partial(lambda off, i: (i + off, 0), off)
        in_specs = [pl.BlockSpec((tm, D_MODEL), full),
                    pl.BlockSpec((tm, D_MODEL), row),
                    pl.BlockSpec((tm, D_MODEL), row),
                    pl.BlockSpec((tm, TOP_K), row),
                    _resident(g_kv.shape),
                    _resident(g_q.shape),
                    _resident(w_kv.shape),
                    _resident(w_q.shape)]
        operands = [h, y0, y1, gtc, g_kv, g_q, w_kv, w_q]
        aliases = {0: 0}
        if kv is not None:
            in_specs += [pl.BlockSpec(memory_space=pl.ANY), pl.BlockSpec(memory_space=pl.ANY)]
            operands += [kv, q]
            aliases = {0: 0, 8: 1, 9: 2}
        h, kv, q = pl.pallas_call(
            functools.partial(_combine_proj_kernel, q_scale=float(q_scale)),
            out_shape=(jax.ShapeDtypeStruct(h.shape, F32),
                       jax.ShapeDtypeStruct((n, w_kv.shape[1]), BF16),
                       jax.ShapeDtypeStruct((n, w_q.shape[1]), BF16)),
            grid=(y0.shape[0] // tm,),
            in_specs=in_specs,
            out_specs=(pl.BlockSpec((tm, D_MODEL), full),
                       pl.BlockSpec((tm, w_kv.shape[1]), full),
                       pl.BlockSpec((tm, w_q.shape[1]), full)),
            input_output_aliases=aliases,
            compiler_params=_params("parallel"),
            name="combine_proj",
        )(*operands)
    return h, kv, q


def _combine_norm_kernel(h_ref, y0_ref, y1_ref, gt_ref, g_ref, o_ref):
    o_ref[...] = _rms(_mix(h_ref, y0_ref, y1_ref, gt_ref), g_ref[...])


def _combine_norm(h, y_sorted, pos, gt, g):
    tm = TM_PROJ
    for row0, y0, y1, gtc in _gather_chunks(y_sorted, pos, gt):
        off = row0 // tm
        row = lambda i: (i, 0)
        full = functools.partial(lambda off, i: (i + off, 0), off)
        h = pl.pallas_call(
            _combine_norm_kernel,
            out_shape=jax.ShapeDtypeStruct(h.shape, F32),
            grid=(y0.shape[0] // tm,),
            in_specs=[pl.BlockSpec((tm, D_MODEL), full),
                      pl.BlockSpec((tm, D_MODEL), row),
                      pl.BlockSpec((tm, D_MODEL), row),
                      pl.BlockSpec((tm, TOP_K), row),
                      _resident(g.shape)],
            out_specs=pl.BlockSpec((tm, D_MODEL), full),
            input_output_aliases={0: 0},
            compiler_params=_params("parallel"),
            name="combine_norm",
        )(h, y0, y1, gtc, g)
    return h


def _sb_attn_kernel(q_ref, k_ref, v_ref, u_ref, o_ref, *, tb, heads):
    tri = u_ref[...]
    nq = q_ref.shape[1] // tb
    below_diag = (lax.broadcasted_iota(jnp.int32, (tb, tb), 1)
                  < lax.broadcasted_iota(jnp.int32, (tb, tb), 0))

    def block(q, g, jb, r_sum, acc, diagonal):
        start = _aligned(jb * tb, tb)
        kb = k_ref[0, pl.ds(start, tb), g * SB_HEAD:(g + 1) * SB_HEAD]
        vb = v_ref[0, pl.ds(start, tb), g * SB_HEAD:(g + 1) * SB_HEAD]
        z = _dot_nt(q, kb)
        neg_abs = pltpu.bitcast(pltpu.bitcast(z, jnp.uint32) | jnp.uint32(0x80000000), F32)
        soft = jnp.log(1.0 + jnp.exp2(neg_abs)) * LOG2E
        ls = jnp.minimum(z, 0.0) - soft
        lb = ls - z
        if diagonal:
            lb = jnp.where(below_diag, lb, 0.0)
        between = _dot(lb.astype(BF16), tri)
        a = jnp.exp2(ls + between + r_sum)
        if diagonal:
            a = jnp.where(below_diag, a, 0.0)
        acc = acc + _dot(a.astype(BF16), vb)
        r_sum = r_sum + jnp.sum(lb, axis=-1, keepdims=True)
        return r_sum, acc

    def q_block(i, first):
        qs = _aligned(i * tb, tb)
        qg = [q_ref[0, pl.ds(qs, tb), g * SB_HEAD:(g + 1) * SB_HEAD] for g in range(heads)]
        zero_r, zero_acc = jnp.zeros((tb, 1), F32), jnp.zeros((tb, SB_HEAD), F32)
        state = [block(qg[g], g, i, zero_r, zero_acc, True) for g in range(heads)]
        if not first:
            state = [block(qg[g], g, i - 1, state[g][0], state[g][1], False) for g in range(heads)]

            def cond(c):
                live = jnp.max(c[1][0][0])
                for g in range(1, heads):
                    live = jnp.maximum(live, jnp.max(c[1][g][0]))
                return jnp.logical_and(c[0] >= 0, live > SB_EXIT_LOG2)

            def body(c):
                return c[0] - 1, [block(qg[g], g, c[0], c[1][g][0], c[1][g][1], False) for g in range(heads)]

            _, state = lax.while_loop(cond, body, (i - 2, state))
        for g in range(heads):
            o_ref[0, pl.ds(qs, tb), g * SB_HEAD:(g + 1) * SB_HEAD] = state[g][1].astype(BF16)

    q_block(0, True)

    def rest(i, carry):
        q_block(i, False)
        return carry

    lax.fori_loop(1, nq, rest, 0)


def _sb_attn(q, kv, tri):
    b, s, _ = q.shape
    hg = SB_HEADS_PER_STEP
    ng = SB_HEADS // hg
    w = hg * SB_HEAD
    return pl.pallas_call(
        functools.partial(_sb_attn_kernel, tb=TB_SB, heads=hg),
        out_shape=jax.ShapeDtypeStruct((b, s, SB_HEADS * SB_HEAD), BF16),
        grid=(b, ng),
        in_specs=[pl.BlockSpec((1, s, w), lambda bi, h: (bi, 0, h)),
                  pl.BlockSpec((1, s, w), lambda bi, h: (bi, 0, h)),
                  pl.BlockSpec((1, s, w), lambda bi, h: (bi, 0, ng + h)),
                  pl.BlockSpec((TB_SB, TB_SB), lambda bi, h: (0, 0))],
        out_specs=pl.BlockSpec((1, s, w), lambda bi, h: (bi, 0, h)),
        compiler_params=_params("parallel", "parallel"),
        name="sb_attn",
    )(q, kv, kv, tri)


def _rope_table(positions):
    inv_freq = 1.0 / (ROPE_BASE ** (jnp.arange(0, QK_ROPE, 2, dtype=F32) / QK_ROPE))
    ang = positions.astype(F32).reshape(-1, 1) * jnp.tile(inv_freq, LANES // inv_freq.shape[0])
    lane = lax.broadcasted_iota(jnp.int32, ang.shape, 1)
    return jnp.where(lane < QK_ROPE, jnp.cos(ang), jnp.sin(ang))


def _mla_weights(w_down, w_uq, w_ukv):
    wd = jnp.pad(w_down, ((0, 0), (0, DOWN_PAD - w_down.shape[1]))).astype(BF16)
    wuq = w_uq.reshape(Q_LORA, MLA_HEADS, QK_NOPE + QK_ROPE)
    wuq = jnp.pad(wuq, ((0, 0), (0, 0), (0, MLA_QK_PAD - QK_NOPE - QK_ROPE)))
    wuq = wuq.reshape(Q_LORA, MLA_HEADS * MLA_QK_PAD).astype(BF16)
    wukv = w_ukv.reshape(KV_LORA, MLA_HEADS, QK_NOPE + V_HEAD)
    wuk = wukv[:, :, :QK_NOPE].reshape(KV_LORA, MLA_HEADS * QK_NOPE).astype(BF16)
    wuv = wukv[:, :, QK_NOPE:].reshape(KV_LORA, MLA_HEADS * V_HEAD).astype(BF16)
    return wd, wuq, wuk, wuv


def kernel(x, positions, attn_norm_0, mla_w_down_0, mla_q_norm_0, mla_w_uq_0, mla_kv_norm_0, mla_w_ukv_0, mla_w_o_0, ffn_norm_0, ffn_w_gate_up_0, ffn_w_down_0, attn_norm_1, mla_w_down_1, mla_q_norm_1, mla_w_uq_1, mla_kv_norm_1, mla_w_ukv_1, mla_w_o_1, ffn_norm_1, moe_router_1, moe_w_gate_up_1, moe_w_down_1, kv_shared_norm, kv_shared_w, attn_norm_2, sb_w_q_2, sb_w_o_2, ffn_norm_2, ffn_w_gate_up_2, ffn_w_down_2, attn_norm_3, sb_w_q_3, sb_w_o_3, ffn_norm_3, moe_router_3, moe_w_gate_up_3, moe_w_down_3, final_norm):
    b, s, d = x.shape
    n = b * s
    row = lambda g: g.reshape(1, -1)
    h = x.reshape(n, d)
    trig = _rope_table(positions)

    mla_layers = [
        (attn_norm_0, mla_w_down_0, mla_q_norm_0, mla_w_uq_0, mla_kv_norm_0, mla_w_ukv_0, mla_w_o_0),
        (attn_norm_1, mla_w_down_1, mla_q_norm_1, mla_w_uq_1, mla_kv_norm_1, mla_w_ukv_1, mla_w_o_1),
    ]
    tri = (lax.broadcasted_iota(jnp.int32, (TB_SB, TB_SB), 0)
           > lax.broadcasted_iota(jnp.int32, (TB_SB, TB_SB), 1)).astype(BF16)
    sb_scale = SB_HEAD ** -0.5 * LOG2E

    def ffn_weights(w_gate_up, w_down):
        return _to_bf16_halves(w_gate_up, 4) + _to_bf16_halves(w_down, 2)

    def mla_attention(h, layer, cast_gate_up, cast_down):
        an, w_down, qn, w_uq, kvn, w_ukv, _ = mla_layers[layer]
        wd, wuq, wuk, wuv = _mla_weights(w_down, w_uq, w_ukv)
        q, k, v = _mla_proj(h, row(an), wd, row(qn), row(kvn), wuq, wuk, wuv, trig)
        o, cast = _mla_attn(q.reshape(b, s, -1), k.reshape(b, s, -1), v.reshape(b, s, -1), cast_gate_up, cast_down)
        return o.reshape(n, -1), cast

    o, moe_w_1 = mla_attention(h, 0, moe_w_gate_up_1, moe_w_down_1)
    h = _dense_ffn(h, o, mla_w_o_0.astype(BF16), row(ffn_norm_0), ffn_weights(ffn_w_gate_up_0, ffn_w_down_0), False)
    o, moe_w_3 = mla_attention(h, 1, moe_w_gate_up_3, moe_w_down_3)
    h, y_sorted, pos, gt = _moe_dispatch(h, o, mla_w_o_1.astype(BF16), row(ffn_norm_1), moe_router_1, moe_w_1)
    h, kv_shared, q = _combine_proj(h, y_sorted, pos, gt, row(kv_shared_norm), row(attn_norm_2),
                                    kv_shared_w.astype(BF16), sb_w_q_2.astype(BF16), sb_scale)
    kv_shared = kv_shared.reshape(b, s, -1)
    o = _sb_attn(q.reshape(b, s, -1), kv_shared, tri).reshape(n, -1)
    h, q = _dense_ffn(h, o, sb_w_o_2.astype(BF16), row(ffn_norm_2), ffn_weights(ffn_w_gate_up_2, ffn_w_down_2), True,
                      q_proj=(row(attn_norm_3), sb_w_q_3.astype(BF16), sb_scale))
    o = _sb_attn(q.reshape(b, s, -1), kv_shared, tri).reshape(n, -1)
    h, y_sorted, pos, gt = _moe_dispatch(h, o, sb_w_o_3.astype(BF16), row(ffn_norm_3), moe_router_3, moe_w_3)
    return _combine_norm(h, y_sorted, pos, gt, row(final_norm)).reshape(b, s, d)
```

```python
import functools

import jax
import jax.numpy as jnp
from jax import lax
from jax.experimental import pallas as pl
from jax.experimental.pallas import tpu as pltpu

F32 = jnp.float32
BF16 = jnp.bfloat16

D_MODEL = 1024
CHUNK = 64
MLA_HEADS = 8
QK_NOPE = 128
QK_ROPE = 64
V_HEAD = 128
Q_LORA = 384
KV_LORA = 256
ROPE_BASE = 10000.0
SB_HEADS = 8
SB_HEAD = 128
D_FF = 2816
N_EXPERTS = 8
TOP_K = 2
EPS = 1e-6

LANES = 128
MLA_QK_PAD = 256
DOWN_PAD = 768
VMEM_LIMIT = 56 * 1024 * 1024

TM_PROJ = 512
TM_FFN = 512
TM_MOE = 256
MOE_CHUNKS = 4
COMBINE_CHUNKS = 4
CAST_BLOCK_BYTES = 12 * 1024 * 1024
TQ_MLA = 512
TK_MLA = 512
MLA_ROW_CHUNK = 32
MLA_HEADS_PER_STEP = 4
TB_SB = 256
SB_HEADS_PER_STEP = 4
TM_ROUTER = 512
LOG2E = 1.4426950408889634
SB_EXIT_LOG2 = -150.0


def _rms(x, g):
    ms = jnp.mean(x * x, axis=-1, keepdims=True)
    return x * lax.rsqrt(ms + EPS) * g


def _dot(a, b):
    return jnp.dot(a, b, preferred_element_type=F32)


def _dot_nt(a, b):
    return lax.dot_general(a, b, (((1,), (1,)), ((), ())), preferred_element_type=F32)


def _silu(g):
    return g / (1.0 + jnp.exp(-g))


def _aligned(x, m):
    return x if isinstance(x, int) else pl.multiple_of(x, m)


def _params(*sem):
    return pltpu.CompilerParams(dimension_semantics=sem, vmem_limit_bytes=VMEM_LIMIT)


def _mla_proj_kernel(h_ref, g_ref, wd_ref, qn_ref, kvn_ref, wuq_ref, wuk_ref, wuv_ref,
                     trig_ref, q_ref, k_ref, v_ref, *, scale):
    xn = _rms(h_ref[...], g_ref[...]).astype(BF16)
    c = _dot(xn, wd_ref[...])
    cq = _rms(c[:, :Q_LORA], qn_ref[...]).astype(BF16)
    ckv = _rms(c[:, Q_LORA:Q_LORA + KV_LORA], kvn_ref[...]).astype(BF16)
    trig = trig_ref[...]
    swapped = pltpu.roll(trig, QK_ROPE, 1)
    lane = lax.broadcasted_iota(jnp.int32, trig.shape, 1)
    cos_t = jnp.where(lane < QK_ROPE, trig, 0.0)
    sin_a = jnp.where(lane < QK_ROPE // 2, -swapped, 0.0)
    sin_b = jnp.where(jnp.logical_and(lane >= QK_ROPE // 2, lane < QK_ROPE), swapped, 0.0)

    def rope(r):
        return r * cos_t + pltpu.roll(r, 96, 1) * sin_a + pltpu.roll(r, 32, 1) * sin_b

    kr = rope(c[:, Q_LORA + KV_LORA:]).astype(BF16)
    q = _dot(cq, wuq_ref[...])
    kn = _dot(ckv, wuk_ref[...])
    v_ref[...] = _dot(ckv, wuv_ref[...]).astype(BF16)
    for h in range(MLA_HEADS):
        lo = h * MLA_QK_PAD
        q_ref[:, lo:lo + LANES] = (q[:, lo:lo + LANES] * scale).astype(BF16)
        q_ref[:, lo + LANES:lo + 2 * LANES] = (rope(q[:, lo + LANES:lo + 2 * LANES]) * scale).astype(BF16)
        k_ref[:, lo:lo + LANES] = kn[:, h * LANES:(h + 1) * LANES].astype(BF16)
        k_ref[:, lo + LANES:lo + 2 * LANES] = kr


def _mla_proj(h, g, wd, qn, kvn, wuq, wuk, wuv, trig):
    n = h.shape[0]
    tm = TM_PROJ
    row = lambda i: (i, 0)
    fixed = lambda i: (0, 0)
    scale = float((QK_NOPE + QK_ROPE) ** -0.5 * LOG2E)
    return pl.pallas_call(
        functools.partial(_mla_proj_kernel, scale=scale),
        out_shape=(jax.ShapeDtypeStruct((n, MLA_HEADS * MLA_QK_PAD), BF16),
                   jax.ShapeDtypeStruct((n, MLA_HEADS * MLA_QK_PAD), BF16),
                   jax.ShapeDtypeStruct((n, MLA_HEADS * V_HEAD), BF16)),
        grid=(n // tm,),
        in_specs=[pl.BlockSpec((tm, D_MODEL), row),
                  pl.BlockSpec((1, D_MODEL), fixed),
                  pl.BlockSpec((D_MODEL, DOWN_PAD), fixed),
                  pl.BlockSpec((1, Q_LORA), fixed),
                  pl.BlockSpec((1, KV_LORA), fixed),
                  pl.BlockSpec((Q_LORA, MLA_HEADS * MLA_QK_PAD), fixed),
                  pl.BlockSpec((KV_LORA, MLA_HEADS * QK_NOPE), fixed),
                  pl.BlockSpec((KV_LORA, MLA_HEADS * V_HEAD), fixed),
                  pl.BlockSpec((tm, LANES), row)],
        out_specs=(pl.BlockSpec((tm, MLA_HEADS * MLA_QK_PAD), row),
                   pl.BlockSpec((tm, MLA_HEADS * MLA_QK_PAD), row),
                   pl.BlockSpec((tm, MLA_HEADS * V_HEAD), row)),
        compiler_params=_params("parallel"),
        name="mla_proj",
    )(h, g, wd, qn, kvn, wuq, wuk, wuv, trig)


def _mla_attn_kernel(q_ref, k_ref, v_ref, wgu_ref, wd_ref, o_ref, wg_ref, wu_ref, wda_ref, wdb_ref, ybuf_ref,
                     s_ref, p_ref, m_ref, l_ref, alpha_ref, acc_ref, *, tq, tk, heads):
    wg_ref[...] = wgu_ref[:, :D_FF].astype(BF16)
    wu_ref[...] = wgu_ref[:, D_FF:].astype(BF16)
    wda_ref[...] = wd_ref[:, :D_MODEL // 2].astype(BF16)
    wdb_ref[...] = wd_ref[:, D_MODEL // 2:].astype(BF16)
    ybuf_ref[...] = jnp.zeros(ybuf_ref.shape, BF16)

    i = pl.program_id(2)
    qg = [q_ref[0, :, g * MLA_QK_PAD:(g + 1) * MLA_QK_PAD] for g in range(heads)]
    rc = MLA_ROW_CHUNK

    m_ref[...] = jnp.full(m_ref.shape, -jnp.inf, F32)
    l_ref[...] = jnp.zeros(l_ref.shape, F32)
    acc_ref[...] = jnp.zeros(acc_ref.shape, F32)

    def scores(g, j):
        kb = k_ref[0, pl.ds(_aligned(j * tk, tk), tk), g * MLA_QK_PAD:(g + 1) * MLA_QK_PAD]
        s_ref[g] = _dot_nt(qg[g], kb)

    def update(g, j, diagonal):
        vb = v_ref[0, pl.ds(_aligned(j * tk, tk), tk), g * V_HEAD:(g + 1) * V_HEAD]
        for c in range(tq // rc):
            rows = slice(c * rc, (c + 1) * rc)
            live_tiles = tk // LANES
            if diagonal:
                visible = (((c + 1) * rc - 1) // CHUNK + 1) * CHUNK
                live_tiles = -(-visible // LANES)
                s = s_ref[g, rows, :live_tiles * LANES]
                qc = (lax.broadcasted_iota(jnp.int32, s.shape, 0) + c * rc) // CHUNK
                kc = lax.broadcasted_iota(jnp.int32, s.shape, 1) // CHUNK
                s = jnp.where(kc <= qc, s, -jnp.inf)
                if live_tiles < tk // LANES:
                    p_ref[g, rows, live_tiles * LANES:] = jnp.zeros((rc, tk - live_tiles * LANES), BF16)
            else:
                s = s_ref[g, rows, :]
            m_old = m_ref[g, rows, :]
            m_new = jnp.maximum(m_old, jnp.max(s, axis=-1, keepdims=True))
            alpha = jnp.exp2(m_old - m_new)
            m_ref[g, rows, :] = m_new
            alpha_ref[g, rows, :] = alpha
            psum = jnp.zeros((rc, LANES), F32)
            for t in range(live_tiles):
                p = jnp.exp2(s[:, t * LANES:(t + 1) * LANES] - m_new)
                psum = psum + p
                p_ref[g, rows, t * LANES:(t + 1) * LANES] = p.astype(BF16)
            l_ref[g, rows, :] = alpha * l_ref[g, rows, :] + jnp.sum(psum, axis=-1, keepdims=True)
        acc_ref[g] = alpha_ref[g] * acc_ref[g] + _dot(p_ref[g], vb)

    for g in range(heads):
        scores(g, 0)

    def body(j, carry):
        for g in range(heads):
            update(g, j, False)
            scores(g, j + 1)
        return carry

    lax.fori_loop(0, i, body, 0)
    for g in range(heads):
        update(g, i, True)
    for g in range(heads):
        o_ref[0, :, g * V_HEAD:(g + 1) * V_HEAD] = (acc_ref[g] / l_ref[g]).astype(BF16)


def _mla_attn(q, k, v, w_gate_up, w_down):
    b, s, _ = q.shape
    tq, tk, hg = TQ_MLA, TK_MLA, MLA_HEADS_PER_STEP
    ng, nq = MLA_HEADS // hg, s // tq
    steps = b * ng * nq
    wgu2 = w_gate_up.reshape(-1, 2 * D_FF)
    wd2 = w_down.reshape(-1, D_MODEL)
    r_gu, r_d = wgu2.shape[0] // steps, wd2.shape[0] // steps
    assert r_gu * steps == wgu2.shape[0] and r_d * steps == wd2.shape[0] and r_gu % 16 == 0 and r_d % 16 == 0
    p_rows = TOP_K * b * s + N_EXPERTS * TM_MOE
    r_y = p_rows // steps
    assert r_y * steps == p_rows and r_y % 16 == 0
    slab = lambda bi, h, i: ((bi * ng + h) * nq + i, 0)
    o, wg, wu, wda, wdb, ybuf = pl.pallas_call(
        functools.partial(_mla_attn_kernel, tq=tq, tk=tk, heads=hg),
        out_shape=(jax.ShapeDtypeStruct((b, s, MLA_HEADS * V_HEAD), BF16),
                   jax.ShapeDtypeStruct((wgu2.shape[0], D_FF), BF16),
                   jax.ShapeDtypeStruct((wgu2.shape[0], D_FF), BF16),
                   jax.ShapeDtypeStruct((wd2.shape[0], D_MODEL // 2), BF16),
                   jax.ShapeDtypeStruct((wd2.shape[0], D_MODEL // 2), BF16),
                   jax.ShapeDtypeStruct((p_rows, D_MODEL), BF16)),
        grid=(b, ng, nq),
        in_specs=[pl.BlockSpec((1, tq, hg * MLA_QK_PAD), lambda bi, h, i: (bi, i, h)),
                  pl.BlockSpec((1, s, hg * MLA_QK_PAD), lambda bi, h, i: (bi, 0, h)),
                  pl.BlockSpec((1, s, hg * V_HEAD), lambda bi, h, i: (bi, 0, h)),
                  pl.BlockSpec((r_gu, 2 * D_FF), slab),
                  pl.BlockSpec((r_d, D_MODEL), slab)],
        out_specs=(pl.BlockSpec((1, tq, hg * V_HEAD), lambda bi, h, i: (bi, i, h)),
                   pl.BlockSpec((r_gu, D_FF), slab),
                   pl.BlockSpec((r_gu, D_FF), slab),
                   pl.BlockSpec((r_d, D_MODEL // 2), slab),
                   pl.BlockSpec((r_d, D_MODEL // 2), slab),
                   pl.BlockSpec((r_y, D_MODEL), slab)),
        scratch_shapes=[pltpu.VMEM((hg, tq, tk), F32), pltpu.VMEM((hg, tq, tk), BF16),
                        pltpu.VMEM((hg, tq, LANES), F32), pltpu.VMEM((hg, tq, LANES), F32),
                        pltpu.VMEM((hg, tq, LANES), F32), pltpu.VMEM((hg, tq, V_HEAD), F32)],
        compiler_params=_params("parallel", "parallel", "arbitrary"),
        name="mla_attn",
    )(q, k, v, wgu2, wd2)
    e = w_gate_up.shape[0]
    ffn_w = (wg.reshape(e, D_MODEL, D_FF), wu.reshape(e, D_MODEL, D_FF),
             wda.reshape(e, D_FF, D_MODEL // 2), wdb.reshape(e, D_FF, D_MODEL // 2))
    return o, ffn_w, ybuf


def _cast_kernel(*refs, n_in, n_out):
    per = n_in // n_out
    for o in range(n_out):
        parts = [refs[o * per + k][...].astype(BF16) for k in range(per)]
        refs[n_in + o][...] = parts[0] if per == 1 else jnp.concatenate(parts, axis=1)


def _to_bf16_halves(w, n_in):
    n_out = 2
    cols = w.shape[-1]
    rows = w.size // cols
    wi, wo = cols // n_in, cols // n_out
    tr = max(r for r in range(16, rows + 1, 16) if rows % r == 0 and r * cols * 4 <= CAST_BLOCK_BYTES)
    outs = pl.pallas_call(
        functools.partial(_cast_kernel, n_in=n_in, n_out=n_out),
        out_shape=tuple(jax.ShapeDtypeStruct((rows, wo), BF16) for _ in range(n_out)),
        grid=(rows // tr,),
        in_specs=[pl.BlockSpec((tr, wi), functools.partial(lambda c, i: (i, c), c)) for c in range(n_in)],
        out_specs=tuple(pl.BlockSpec((tr, wo), lambda i: (i, 0)) for _ in range(n_out)),
        compiler_params=_params("parallel"),
        name="cast_bf16",
    )(*([w.reshape(rows, cols)] * n_in))
    return tuple(o.reshape(w.shape[:-1] + (wo,)) for o in outs)


def _swiglu(x, w_refs):
    wg_ref, wu_ref, wda_ref, wdb_ref = w_refs
    gate = _dot(x, wg_ref[...])
    up = _dot(x, wu_ref[...])
    act = (_silu(gate) * up).astype(BF16)
    return jnp.concatenate([_dot(act, wda_ref[...]), _dot(act, wdb_ref[...])], axis=1)


def _ffn_kernel(h_ref, a_ref, wo_ref, g_ref, wg_ref, wu_ref, wda_ref, wdb_ref, *rest, q_scale):
    h = h_ref[...] + _dot(a_ref[...], wo_ref[...])
    x = _rms(h, g_ref[...]).astype(BF16)
    h = h + _swiglu(x, (wg_ref, wu_ref, wda_ref, wdb_ref))
    if q_scale is None:
        (o_ref,) = rest
    else:
        gq_ref, wq_ref, o_ref, q_ref = rest
        q_ref[...] = (_dot(_rms(h, gq_ref[...]).astype(BF16), wq_ref[...]) * q_scale).astype(BF16)
    o_ref[...] = h


def _resident(shape):
    return pl.BlockSpec(shape, lambda *_: (0,) * len(shape), pipeline_mode=pl.Buffered(1))


def _dense_ffn(h, a, w_o, g, ffn_w, in_place, q_proj=None):
    n = h.shape[0]
    tm = TM_FFN
    row = lambda i: (i, 0)
    in_specs = [pl.BlockSpec((tm, D_MODEL), row),
                pl.BlockSpec((tm, D_MODEL), row),
                _resident(w_o.shape),
                _resident(g.shape)] + [_resident(w.shape) for w in ffn_w]
    operands = [h, a, w_o, g, *ffn_w]
    out_shape = [jax.ShapeDtypeStruct(h.shape, F32)]
    out_specs = [pl.BlockSpec((tm, D_MODEL), row)]
    q_scale = None
    if q_proj is not None:
        g_q, w_q, q_scale = q_proj
        in_specs += [_resident(g_q.shape), _resident(w_q.shape)]
        operands += [g_q, w_q]
        out_shape.append(jax.ShapeDtypeStruct((n, w_q.shape[1]), BF16))
        out_specs.append(pl.BlockSpec((tm, w_q.shape[1]), row))
        q_scale = float(q_scale)
    out = pl.pallas_call(
        functools.partial(_ffn_kernel, q_scale=q_scale),
        out_shape=tuple(out_shape),
        grid=(n // tm,),
        in_specs=in_specs,
        out_specs=tuple(out_specs),
        input_output_aliases={0: 0} if in_place else {},
        compiler_params=_params("parallel"),
        name="dense_ffn",
    )(*operands)
    return out[0] if q_proj is None else out


def _router_kernel(h_ref, a_ref, wo_ref, g_ref, rhi_ref, rlo_ref, tri_ref,
                   h_out_ref, xn_ref, idx_ref, gate_ref, rank_ref, count_ref, base_ref):
    h = h_ref[...] + _dot(a_ref[...], wo_ref[...])
    h_out_ref[...] = h
    xn = _rms(h, g_ref[...])
    x_hi = xn.astype(BF16)
    x_lo = (xn - x_hi.astype(F32)).astype(BF16)
    xn_ref[...] = x_hi
    logits = _dot_nt(rhi_ref[...], x_hi) + _dot_nt(rhi_ref[...], x_lo) + _dot_nt(rlo_ref[...], x_hi)
    m = jnp.max(logits, axis=0, keepdims=True)
    e = jnp.exp(logits - m)
    p = e / jnp.sum(e, axis=0, keepdims=True)
    eid = lax.broadcasted_iota(jnp.int32, p.shape, 0)
    p1 = jnp.max(p, axis=0, keepdims=True)
    i1 = jnp.min(jnp.where(p == p1, eid, N_EXPERTS), axis=0, keepdims=True)
    pm = jnp.where(eid == i1, -1.0, p)
    p2 = jnp.max(pm, axis=0, keepdims=True)
    i2 = jnp.min(jnp.where(pm == p2, eid, N_EXPERTS), axis=0, keepdims=True)
    den = p1 + p2
    idx_ref[...] = jnp.concatenate([i1, i2], axis=0)
    gate_ref[...] = jnp.concatenate([p1 / den, p2 / den], axis=0)

    @pl.when(pl.program_id(0) == 0)
    def _():
        base_ref[...] = jnp.zeros(base_ref.shape, F32)

    first = (eid == i1).astype(F32)
    second = (eid == i2).astype(F32)
    earlier = tri_ref[...]
    before_first = _dot(first.astype(BF16), earlier)
    before_second = _dot(second.astype(BF16), earlier)
    n_first = jnp.sum(first, axis=1, keepdims=True)
    n_second = jnp.sum(second, axis=1, keepdims=True)
    base = base_ref[:, 0:1]
    rank_first = jnp.sum(first * (base + before_first), axis=0, keepdims=True)
    rank_second = jnp.sum(second * (base + n_first + before_second), axis=0, keepdims=True)
    rank_ref[...] = jnp.concatenate([rank_first, rank_second], axis=0).astype(jnp.int32)
    total = jnp.broadcast_to(base + n_first + n_second, base_ref.shape)
    base_ref[...] = total
    count_ref[...] = total.astype(jnp.int32)


def _router(h, a, w_o, g, r_hi, r_lo):
    n = h.shape[0]
    tm = TM_ROUTER
    row = lambda i: (i, 0)
    col = lambda i: (0, i)
    earlier = (lax.broadcasted_iota(jnp.int32, (tm, tm), 0)
               < lax.broadcasted_iota(jnp.int32, (tm, tm), 1)).astype(BF16)
    return pl.pallas_call(
        _router_kernel,
        out_shape=(jax.ShapeDtypeStruct((n, D_MODEL), F32),
                   jax.ShapeDtypeStruct((n, D_MODEL), BF16),
                   jax.ShapeDtypeStruct((TOP_K, n), jnp.int32),
                   jax.ShapeDtypeStruct((TOP_K, n), F32),
                   jax.ShapeDtypeStruct((TOP_K, n), jnp.int32),
                   jax.ShapeDtypeStruct((N_EXPERTS, LANES), jnp.int32)),
        grid=(n // tm,),
        in_specs=[pl.BlockSpec((tm, D_MODEL), row),
                  pl.BlockSpec((tm, D_MODEL), row),
                  _resident(w_o.shape),
                  _resident(g.shape),
                  _resident(r_hi.shape),
                  _resident(r_lo.shape),
                  _resident(earlier.shape)],
        out_specs=(pl.BlockSpec((tm, D_MODEL), row),
                   pl.BlockSpec((tm, D_MODEL), row),
                   pl.BlockSpec((TOP_K, tm), col),
                   pl.BlockSpec((TOP_K, tm), col),
                   pl.BlockSpec((TOP_K, tm), col),
                   pl.BlockSpec((N_EXPERTS, LANES), lambda i: (0, 0))),
        scratch_shapes=[pltpu.VMEM((N_EXPERTS, LANES), F32)],
        input_output_aliases={0: 0},
        compiler_params=_params("arbitrary"),
        name="moe_router",
    )(h, a, w_o, g, r_hi, r_lo, earlier)


def _moe_kernel(te_ref, nu_ref, x_ref, wg_ref, wu_ref, wda_ref, wdb_ref, y_ref, o_ref, *, tile0):
    del y_ref
    used = pl.program_id(0) + tile0 < nu_ref[0]

    @pl.when(used)
    def _():
        o_ref[...] = _swiglu(x_ref[...], (wg_ref.at[0], wu_ref.at[0], wda_ref.at[0], wdb_ref.at[0])).astype(BF16)

    @pl.when(jnp.logical_not(used))
    def _():
        o_ref[...] = jnp.zeros(o_ref.shape, BF16)


def _moe_experts(tile_expert, n_used, x_chunk, ffn_w, y_prev, tile0):
    tm = TM_MOE
    p_rows = y_prev.shape[0]
    in_specs = ([pl.BlockSpec((tm, D_MODEL), lambda t, te, nu: (t, 0))]
                + [pl.BlockSpec((1,) + w.shape[1:], lambda t, te, nu: (te[t + tile0], 0, 0)) for w in ffn_w]
                + [pl.BlockSpec(memory_space=pl.ANY)])
    operands = [tile_expert, n_used, x_chunk, *ffn_w, y_prev]
    aliases = {len(operands) - 1: 0}
    grid_spec = pltpu.PrefetchScalarGridSpec(
        num_scalar_prefetch=2,
        grid=(x_chunk.shape[0] // tm,),
        in_specs=in_specs,
        out_specs=pl.BlockSpec((tm, D_MODEL), lambda t, te, nu: (t + tile0, 0)),
    )
    return pl.pallas_call(
        functools.partial(_moe_kernel, tile0=tile0),
        out_shape=jax.ShapeDtypeStruct((p_rows, D_MODEL), BF16),
        grid_spec=grid_spec,
        input_output_aliases=aliases,
        compiler_params=_params("arbitrary"),
        name="moe_experts",
    )(*operands)


def _moe_dispatch(h, a, w_o, g, router, ffn_w, y_sorted):
    n = h.shape[0]
    tm, tr = TM_MOE, TM_ROUTER
    r_t = router.T
    r_hi = r_t.astype(BF16)
    r_lo = (r_t - r_hi.astype(F32)).astype(BF16)
    h, xn, idx, gates, rank, count = _router(h, a, w_o, g, r_hi, r_lo)

    experts = jnp.arange(N_EXPERTS, dtype=jnp.int32)
    counts = count[:, 0]
    padded = ((counts + tm - 1) // tm) * tm
    ends = jnp.cumsum(padded)
    offsets = ends - padded
    starts = jnp.cumsum(counts) - counts
    pos = rank + jnp.sum(jnp.where(idx[None] == experts[:, None, None], offsets[:, None, None], 0), axis=0)
    p_rows = TOP_K * n + N_EXPERTS * tm
    n_tiles = p_rows // tm
    n_used = (ends[-1] // tm).astype(jnp.int32)
    tile_start = jnp.arange(n_tiles, dtype=jnp.int32) * tm
    tile_expert = jnp.sum((tile_start[:, None] >= ends[None, :]).astype(jnp.int32), axis=1)
    tile_expert = jnp.minimum(tile_expert, N_EXPERTS - 1)
    last_expert = tile_expert[jnp.maximum(n_used - 1, 0)]
    tile_expert = jnp.where(jnp.arange(n_tiles) < n_used, tile_expert, last_expert)
    e_flat = idx.reshape(TOP_K, n // tr, tr).transpose(1, 0, 2).reshape(-1)
    order = jnp.argsort(e_flat, stable=True).astype(jnp.int32)
    order_tok = (order // (TOP_K * tr)) * tr + order % tr
    within = jnp.arange(p_rows, dtype=jnp.int32) - jnp.repeat(offsets[tile_expert], tm)
    src = jnp.repeat(starts[tile_expert], tm) + within
    valid = within < jnp.repeat(counts[tile_expert], tm)
    tok_sorted = jnp.where(valid, order_tok[jnp.clip(src, 0, TOP_K * n - 1)], 0)
    assert y_sorted.shape == (p_rows, D_MODEL)
    chunk_tiles = n_tiles // MOE_CHUNKS
    for c in range(MOE_CHUNKS):
        rows = slice(c * chunk_tiles * tm, (c + 1) * chunk_tiles * tm)
        x_chunk = jnp.take(xn, tok_sorted[rows], axis=0, mode="clip")
        y_sorted = _moe_experts(tile_expert, n_used.reshape(1), x_chunk, ffn_w, y_sorted, c * chunk_tiles)
    return h, y_sorted, pos, gates.T


def _mix(h_ref, y0_ref, y1_ref, gt_ref):
    gt = gt_ref[...]
    return h_ref[...] + gt[:, 0:1] * y0_ref[...].astype(F32) + gt[:, 1:2] * y1_ref[...].astype(F32)


def _gather_chunks(y_sorted, pos, gt):
    n = pos.shape[1]
    cn = n // COMBINE_CHUNKS
    for c in range(COMBINE_CHUNKS):
        rows = slice(c * cn, (c + 1) * cn)
        yield (c * cn,
               jnp.take(y_sorted, pos[0, rows], axis=0, mode="clip"),
               jnp.take(y_sorted, pos[1, rows], axis=0, mode="clip"),
               gt[rows])


def _combine_proj_kernel(h_ref, y0_ref, y1_ref, gt_ref, gk_ref, gq_ref, wkv_ref, wq_ref, *rest, q_scale):
    h_out_ref, kv_ref, q_ref = rest[-3:]
    h = _mix(h_ref, y0_ref, y1_ref, gt_ref)
    h_out_ref[...] = h
    xhat = h * lax.rsqrt(jnp.mean(h * h, axis=-1, keepdims=True) + EPS)
    kv_ref[...] = _dot((xhat * gk_ref[...]).astype(BF16), wkv_ref[...]).astype(BF16)
    q_ref[...] = (_dot((xhat * gq_ref[...]).astype(BF16), wq_ref[...]) * q_scale).astype(BF16)


def _combine_proj(h, y_sorted, pos, gt, g_kv, g_q, w_kv, w_q, q_scale):
    n = h.shape[0]
    tm = TM_PROJ
    kv = q = None
    for row0, y0, y1, gtc in _gather_chunks(y_sorted, pos, gt):
        off = row0 // tm
        row = lambda i: (i, 0)
        full = functools.partial(lambda off, i: (i + off, 0), off)
        in_specs = [pl.BlockSpec((tm, D_MODEL), full),
                    pl.BlockSpec((tm, D_MODEL), row),
                    pl.BlockSpec((tm, D_MODEL), row),
                    pl.BlockSpec((tm, TOP_K), row),
                    _resident(g_kv.shape),
                    _resident(g_q.shape),
                    _resident(w_kv.shape),
                    _resident(w_q.shape)]
        operands = [h, y0, y1, gtc, g_kv, g_q, w_kv, w_q]
        aliases = {0: 0}
        if kv is not None:
            in_specs += [pl.BlockSpec(memory_space=pl.ANY), pl.BlockSpec(memory_space=pl.ANY)]
            operands += [kv, q]
            aliases = {0: 0, 8: 1, 9: 2}
        h, kv, q = pl.pallas_call(
            functools.partial(_combine_proj_kernel, q_scale=float(q_scale)),
            out_shape=(jax.ShapeDtypeStruct(h.shape, F32),
                       jax.ShapeDtypeStruct((n, w_kv.shape[1]), BF16),
                       jax.ShapeDtypeStruct((n, w_q.shape[1]), BF16)),
            grid=(y0.shape[0] // tm,),
            in_specs=in_specs,
            out_specs=(pl.BlockSpec((tm, D_MODEL), full),
                       pl.BlockSpec((tm, w_kv.shape[1]), full),
                       pl.BlockSpec((tm, w_q.shape[1]), full)),
            input_output_aliases=aliases,
            compiler_params=_params("parallel"),
            name="combine_proj",
        )(*operands)
    return h, kv, q


def _combine_norm_kernel(h_ref, y0_ref, y1_ref, gt_ref, g_ref, o_ref):
    o_ref[...] = _rms(_mix(h_ref, y0_ref, y1_ref, gt_ref), g_ref[...])


def _combine_norm(h, y_sorted, pos, gt, g):
    tm = TM_PROJ
    for row0, y0, y1, gtc in _gather_chunks(y_sorted, pos, gt):
        off = row0 // tm
        row = lambda i: (i, 0)
        full = functools.partial(lambda off, i: (i + off, 0), off)
        h = pl.pallas_call(
            _combine_norm_kernel,
            out_shape=jax.ShapeDtypeStruct(h.shape, F32),
            grid=(y0.shape[0] // tm,),
            in_specs=[pl.BlockSpec((tm, D_MODEL), full),
                      pl.BlockSpec((tm, D_MODEL), row),
                      pl.BlockSpec((tm, D_MODEL), row),
                      pl.BlockSpec((tm, TOP_K), row),
                      _resident(g.shape)],
            out_specs=pl.BlockSpec((tm, D_MODEL), full),
            input_output_aliases={0: 0},
            compiler_params=_params("parallel"),
            name="combine_norm",
        )(h, y0, y1, gtc, g)
    return h


def _sb_attn_kernel(q_ref, k_ref, v_ref, u_ref, o_ref, *, tb, heads):
    tri = u_ref[...]
    nq = q_ref.shape[1] // tb
    below_diag = (lax.broadcasted_iota(jnp.int32, (tb, tb), 1)
                  < lax.broadcasted_iota(jnp.int32, (tb, tb), 0))

    def block(q, g, jb, r_sum, acc, diagonal):
        start = _aligned(jb * tb, tb)
        kb = k_ref[0, pl.ds(start, tb), g * SB_HEAD:(g + 1) * SB_HEAD]
        vb = v_ref[0, pl.ds(start, tb), g * SB_HEAD:(g + 1) * SB_HEAD]
        z = _dot_nt(q, kb)
        neg_abs = pltpu.bitcast(pltpu.bitcast(z, jnp.uint32) | jnp.uint32(0x80000000), F32)
        soft = jnp.log(1.0 + jnp.exp2(neg_abs)) * LOG2E
        ls = jnp.minimum(z, 0.0) - soft
        lb = ls - z
        if diagonal:
            lb = jnp.where(below_diag, lb, 0.0)
        between = _dot(lb.astype(BF16), tri)
        a = jnp.exp2(ls + between + r_sum)
        if diagonal:
            a = jnp.where(below_diag, a, 0.0)
        acc = acc + _dot(a.astype(BF16), vb)
        r_sum = r_sum + jnp.sum(lb, axis=-1, keepdims=True)
        return r_sum, acc

    def q_block(i, first):
        qs = _aligned(i * tb, tb)
        qg = [q_ref[0, pl.ds(qs, tb), g * SB_HEAD:(g + 1) * SB_HEAD] for g in range(heads)]
        zero_r, zero_acc = jnp.zeros((tb, 1), F32), jnp.zeros((tb, SB_HEAD), F32)
        state = [block(qg[g], g, i, zero_r, zero_acc, True) for g in range(heads)]
        if not first:
            state = [block(qg[g], g, i - 1, state[g][0], state[g][1], False) for g in range(heads)]

            def cond(c):
                live = jnp.max(c[1][0][0])
                for g in range(1, heads):
                    live = jnp.maximum(live, jnp.max(c[1][g][0]))
                return jnp.logical_and(c[0] >= 0, live > SB_EXIT_LOG2)

            def body(c):
                return c[0] - 1, [block(qg[g], g, c[0], c[1][g][0], c[1][g][1], False) for g in range(heads)]

            _, state = lax.while_loop(cond, body, (i - 2, state))
        for g in range(heads):
            o_ref[0, pl.ds(qs, tb), g * SB_HEAD:(g + 1) * SB_HEAD] = state[g][1].astype(BF16)

    q_block(0, True)

    def rest(i, carry):
        q_block(i, False)
        return carry

    lax.fori_loop(1, nq, rest, 0)


def _sb_attn(q, kv, tri):
    b, s, _ = q.shape
    hg = SB_HEADS_PER_STEP
    ng = SB_HEADS // hg
    w = hg * SB_HEAD
    return pl.pallas_call(
        functools.partial(_sb_attn_kernel, tb=TB_SB, heads=hg),
        out_shape=jax.ShapeDtypeStruct((b, s, SB_HEADS * SB_HEAD), BF16),
        grid=(b, ng),
        in_specs=[pl.BlockSpec((1, s, w), lambda bi, h: (bi, 0, h)),
                  pl.BlockSpec((1, s, w), lambda bi, h: (bi, 0, h)),
                  pl.BlockSpec((1, s, w), lambda bi, h: (bi, 0, ng + h)),
                  pl.BlockSpec((TB_SB, TB_SB), lambda bi, h: (0, 0))],
        out_specs=pl.BlockSpec((1, s, w), lambda bi, h: (bi, 0, h)),
        compiler_params=_params("parallel", "parallel"),
        name="sb_attn",
    )(q, kv, kv, tri)


def _rope_table(positions):
    inv_freq = 1.0 / (ROPE_BASE ** (jnp.arange(0, QK_ROPE, 2, dtype=F32) / QK_ROPE))
    ang = positions.astype(F32).reshape(-1, 1) * jnp.tile(inv_freq, LANES // inv_freq.shape[0])
    lane = lax.broadcasted_iota(jnp.int32, ang.shape, 1)
    return jnp.where(lane < QK_ROPE, jnp.cos(ang), jnp.sin(ang))


def _mla_weights(w_down, w_uq, w_ukv):
    wd = jnp.pad(w_down, ((0, 0), (0, DOWN_PAD - w_down.shape[1]))).astype(BF16)
    wuq = w_uq.reshape(Q_LORA, MLA_HEADS, QK_NOPE + QK_ROPE)
    wuq = jnp.pad(wuq, ((0, 0), (0, 0), (0, MLA_QK_PAD - QK_NOPE - QK_ROPE)))
    wuq = wuq.reshape(Q_LORA, MLA_HEADS * MLA_QK_PAD).astype(BF16)
    wukv = w_ukv.reshape(KV_LORA, MLA_HEADS, QK_NOPE + V_HEAD)
    wuk = wukv[:, :, :QK_NOPE].reshape(KV_LORA, MLA_HEADS * QK_NOPE).astype(BF16)
    wuv = wukv[:, :, QK_NOPE:].reshape(KV_LORA, MLA_HEADS * V_HEAD).astype(BF16)
    return wd, wuq, wuk, wuv


def kernel(x, positions, attn_norm_0, mla_w_down_0, mla_q_norm_0, mla_w_uq_0, mla_kv_norm_0, mla_w_ukv_0, mla_w_o_0, ffn_norm_0, ffn_w_gate_up_0, ffn_w_down_0, attn_norm_1, mla_w_down_1, mla_q_norm_1, mla_w_uq_1, mla_kv_norm_1, mla_w_ukv_1, mla_w_o_1, ffn_norm_1, moe_router_1, moe_w_gate_up_1, moe_w_down_1, kv_shared_norm, kv_shared_w, attn_norm_2, sb_w_q_2, sb_w_o_2, ffn_norm_2, ffn_w_gate_up_2, ffn_w_down_2, attn_norm_3, sb_w_q_3, sb_w_o_3, ffn_norm_3, moe_router_3, moe_w_gate_up_3, moe_w_down_3, final_norm):
    b, s, d = x.shape
    n = b * s
    row = lambda g: g.reshape(1, -1)
    h = x.reshape(n, d)
    trig = _rope_table(positions)

    mla_layers = [
        (attn_norm_0, mla_w_down_0, mla_q_norm_0, mla_w_uq_0, mla_kv_norm_0, mla_w_ukv_0, mla_w_o_0),
        (attn_norm_1, mla_w_down_1, mla_q_norm_1, mla_w_uq_1, mla_kv_norm_1, mla_w_ukv_1, mla_w_o_1),
    ]
    tri = (lax.broadcasted_iota(jnp.int32, (TB_SB, TB_SB), 0)
           > lax.broadcasted_iota(jnp.int32, (TB_SB, TB_SB), 1)).astype(BF16)
    sb_scale = SB_HEAD ** -0.5 * LOG2E

    def ffn_weights(w_gate_up, w_down):
        return _to_bf16_halves(w_gate_up, 4) + _to_bf16_halves(w_down, 2)

    def mla_attention(h, layer, cast_gate_up, cast_down):
        an, w_down, qn, w_uq, kvn, w_ukv, _ = mla_layers[layer]
        wd, wuq, wuk, wuv = _mla_weights(w_down, w_uq, w_ukv)
        q, k, v = _mla_proj(h, row(an), wd, row(qn), row(kvn), wuq, wuk, wuv, trig)
        o, cast, ybuf = _mla_attn(q.reshape(b, s, -1), k.reshape(b, s, -1), v.reshape(b, s, -1),
                                  cast_gate_up, cast_down)
        return o.reshape(n, -1), cast, ybuf

    o, moe_w_1, ybuf_1 = mla_attention(h, 0, moe_w_gate_up_1, moe_w_down_1)
    h = _dense_ffn(h, o, mla_w_o_0.astype(BF16), row(ffn_norm_0), ffn_weights(ffn_w_gate_up_0, ffn_w_down_0), False)
    o, moe_w_3, ybuf_3 = mla_attention(h, 1, moe_w_gate_up_3, moe_w_down_3)
    h, y_sorted, pos, gt = _moe_dispatch(h, o, mla_w_o_1.astype(BF16), row(ffn_norm_1), moe_router_1, moe_w_1, ybuf_1)
    h, kv_shared, q = _combine_proj(h, y_sorted, pos, gt, row(kv_shared_norm), row(attn_norm_2),
                                    kv_shared_w.astype(BF16), sb_w_q_2.astype(BF16), sb_scale)
    kv_shared = kv_shared.reshape(b, s, -1)
    o = _sb_attn(q.reshape(b, s, -1), kv_shared, tri).reshape(n, -1)
    h, q = _dense_ffn(h, o, sb_w_o_2.astype(BF16), row(ffn_norm_2), ffn_weights(ffn_w_gate_up_2, ffn_w_down_2), True,
                      q_proj=(row(attn_norm_3), sb_w_q_3.astype(BF16), sb_scale))
    o = _sb_attn(q.reshape(b, s, -1), kv_shared, tri).reshape(n, -1)
    h, y_sorted, pos, gt = _moe_dispatch(h, o, sb_w_o_3.astype(BF16), row(ffn_norm_3), moe_router_3, moe_w_3, ybuf_3)
    return _combine_norm(h, y_sorted, pos, gt, row(final_norm)).reshape(b, s, d)
```
